```python
import jax, jax.numpy as jnp
from jax import lax
import numpy as np

D_MODEL = 1024
BATCH = 8
SEQ = 4096
DEPTH = 4

N_A = DEPTH // 2
N_B = DEPTH - N_A
POOL_WINDOWS = (2, 4, 8, 16)
POOL_GROUPS = len(POOL_WINDOWS)
GC = D_MODEL // POOL_GROUPS
N_HEADS = 16
HEAD_DIM = D_MODEL // N_HEADS
D_FF = 4 * D_MODEL
Q_BLOCK = 128
EPS = 1e-6

kernel_name = "yoco_pool_stickbreak_trunk"


def rms_norm(x, g):
    xf = x.astype(jnp.float32)
    y = xf * lax.rsqrt(jnp.mean(xf * xf, axis=-1, keepdims=True) + EPS)
    return (y * g.astype(jnp.float32)).astype(x.dtype)


def pool_mixer(h, w_grp, scale):
    B, S, D = h.shape
    hf = h.astype(jnp.float32)
    cp = jnp.concatenate([jnp.zeros((B, 1, D), jnp.float32), lax.cumsum(hf, axis=1)], axis=1)
    pos = jnp.arange(S, dtype=jnp.int32)
    outs = []
    for g, w in enumerate(POOL_WINDOWS):
        sl = slice(g * GC, (g + 1) * GC)
        upper = cp[:, 1:, sl]
        lower = jnp.concatenate([jnp.zeros((B, w - 1, GC), jnp.float32), cp[:, :S + 1 - w, sl]], axis=1)
        count = jnp.minimum(pos + 1, w).astype(jnp.float32)[None, :, None]
        outs.append((upper - lower) / count - hf[:, :, sl])
    y = jnp.stack(outs, axis=2)
    y = jnp.einsum('bsgc,gcd->bsgd', y, w_grp.astype(jnp.float32)).reshape(B, S, D)
    return (y * scale.astype(jnp.float32)).astype(h.dtype)


def stick_breaking_attention(q, k, v):
    S = q.shape[2]
    inv_sqrt_d = 1.0 / float(np.sqrt(HEAD_DIM))
    outs = []
    for i in range(S // Q_BLOCK):
        kv_len = (i + 1) * Q_BLOCK
        q_blk = q[:, :, i * Q_BLOCK:kv_len]
        k_blk = k[:, :, :kv_len]
        v_blk = v[:, :, :kv_len]
        z = jnp.einsum('bhqd,bhkd->bhqk', q_blk, k_blk,
                       preferred_element_type=jnp.float32) * inv_sqrt_d
        t_idx = i * Q_BLOCK + jnp.arange(Q_BLOCK, dtype=jnp.int32)[:, None]
        s_idx = jnp.arange(kv_len, dtype=jnp.int32)[None, :]
        mask = s_idx < t_idx
        log1m = jnp.where(mask, -jax.nn.softplus(z), 0.0)
        excl = lax.cumsum(log1m, axis=3, reverse=True) - log1m
        a = jnp.where(mask, jnp.exp(jax.nn.log_sigmoid(z) + excl), 0.0)
        outs.append(jnp.einsum('bhqk,bhkd->bhqd', a, v_blk.astype(jnp.float32)))
    return jnp.concatenate(outs, axis=2).astype(q.dtype)


def sq_relu_mlp(h, w_up, w_down):
    u = jnp.matmul(h, w_up)
    return jnp.matmul(jnp.square(jax.nn.relu(u)), w_down)


def setup_inputs(seed: int = 0) -> dict:
    key = jax.random.key(seed)
    ks = jax.random.split(key, 16)
    D, HD = D_MODEL, N_HEADS * HEAD_DIM
    nrm = lambda k, shape, fan_in: jax.random.normal(k, shape, jnp.float32) * (fan_in ** -0.5)
    gain = lambda k, shape: 1.0 + 0.05 * jax.random.normal(k, shape, jnp.float32)
    return {
        "x": jax.random.normal(ks[0], (BATCH, SEQ, D), jnp.float32),
        "pool_w": nrm(ks[1], (N_A, POOL_GROUPS, GC, GC), GC),
        "pool_scale": 0.5 + jax.random.uniform(ks[2], (N_A, D), jnp.float32),
        "w_q": nrm(ks[3], (N_B, D, HD), D),
        "w_kv": nrm(ks[4], (D, 2 * HD), D),
        "kv_norm_g": gain(ks[5], (D,)),
        "w_o": nrm(ks[6], (N_B, HD, D), HD),
        "w_up": nrm(ks[7], (DEPTH, D, D_FF), D),
        "w_down": nrm(ks[8], (DEPTH, D_FF, D), D_FF),
        "mix_pre_g": gain(ks[9], (DEPTH, D)),
        "mix_post_g": gain(ks[10], (DEPTH, D)),
        "mlp_pre_g": gain(ks[11], (DEPTH, D)),
        "mlp_post_g": gain(ks[12], (DEPTH, D)),
    }


def reference(x, pool_w, pool_scale, w_q, w_kv, kv_norm_g, w_o, w_up, w_down,
              mix_pre_g, mix_post_g, mlp_pre_g, mlp_post_g):
    B, S, D = x.shape
    HD = N_HEADS * HEAD_DIM
    k = v = None
    for layer in range(DEPTH):
        h = rms_norm(x, mix_pre_g[layer])
        if layer < N_A:
            m = pool_mixer(h, pool_w[layer], pool_scale[layer])
        else:
            j = layer - N_A
            q = jnp.matmul(h, w_q[j]).reshape(B, S, N_HEADS, HEAD_DIM).transpose(0, 2, 1, 3)
            o = stick_breaking_attention(q, k, v)
            m = jnp.matmul(o.transpose(0, 2, 1, 3).reshape(B, S, HD), w_o[j])
        x = x + rms_norm(m, mix_post_g[layer])
        h = rms_norm(x, mlp_pre_g[layer])
        x = x + rms_norm(sq_relu_mlp(h, w_up[layer], w_down[layer]), mlp_post_g[layer])
        if layer == N_A - 1:
            kv = jnp.matmul(rms_norm(x, kv_norm_g), w_kv).reshape(B, S, 2, N_HEADS, HEAD_DIM)
            k = kv[:, :, 0].transpose(0, 2, 1, 3)
            v = kv[:, :, 1].transpose(0, 2, 1, 3)
    return x
```

```python
import functools

import jax
import jax.numpy as jnp
from jax import lax
from jax.experimental import pallas as pl
from jax.experimental.pallas import tpu as pltpu

F32 = jnp.float32
BF16 = jnp.bfloat16

EPS = 1e-6
POOL_WINDOWS = (2, 4, 8, 16)
N_HEADS = 16
HALO = 16
LANES = 128
V7X_VMEM_BYTES = 64 * 1024 * 1024
NEG_BIG = -1e30


def _vmem_limit(est_bytes):
    return int(min(V7X_VMEM_BYTES - 4 * 1024 * 1024, est_bytes * 3 // 2 + 8 * 1024 * 1024))


def _rms(x, g):
    ms = jnp.mean(x * x, axis=-1, keepdims=True)
    return x * lax.rsqrt(ms + EPS) * g


def _const_spec(shape):
    nd = len(shape)
    return pl.BlockSpec(shape, lambda *_: (0,) * nd, pipeline_mode=pl.Buffered(1))


def _mlp_kernel(x_ref, pre_g_ref, post_g_ref, wup_ref, wdn_ref, o_ref, a_ref, y_ref, *, fc, nc):
    d = x_ref.shape[-1]
    d_ff = wup_ref.shape[-1]
    h = _rms(x_ref[...], pre_g_ref[...]).astype(BF16)
    for c in range(d_ff // fc):
        u = jnp.dot(h, wup_ref[:, c * fc:(c + 1) * fc], preferred_element_type=F32)
        r = jnp.maximum(u, 0.0)
        a_ref[:, c * fc:(c + 1) * fc] = (r * r).astype(BF16)
    ss = jnp.zeros((x_ref.shape[0], 1), F32)
    for n in range(d // nc):
        y = jnp.dot(a_ref[...], wdn_ref[:, n * nc:(n + 1) * nc], preferred_element_type=F32)
        y_ref[:, n * nc:(n + 1) * nc] = y
        ss = ss + jnp.sum(y * y, axis=-1, keepdims=True)
    inv = lax.rsqrt(ss * (1.0 / d) + EPS)
    o_ref[...] = x_ref[...] + y_ref[...] * inv * post_g_ref[...]


def _mlp_block(x2, pre_g, post_g, w_up, w_dn, *, tm=512, fc=512, nc=256):
    t, d = x2.shape
    d_ff = w_up.shape[1]
    est = (2 * 2 * tm * d * 4 + 2 * d * d_ff * 2 + tm * d_ff * 2 + tm * d * 4)
    return pl.pallas_call(
        functools.partial(_mlp_kernel, fc=fc, nc=nc),
        grid=(t // tm,),
        in_specs=[
            pl.BlockSpec((tm, d), lambda i: (i, 0)),
            _const_spec((1, d)),
            _const_spec((1, d)),
            _const_spec((d, d_ff)),
            _const_spec((d_ff, d)),
        ],
        out_specs=pl.BlockSpec((tm, d), lambda i: (i, 0)),
        out_shape=jax.ShapeDtypeStruct((t, d), F32),
        scratch_shapes=[pltpu.VMEM((tm, d_ff), BF16), pltpu.VMEM((tm, d), F32)],
        compiler_params=pltpu.CompilerParams(
            dimension_semantics=("arbitrary",), vmem_limit_bytes=_vmem_limit(est)),
        name="mlp_block",
    )(x2, pre_g, post_g, w_up, w_dn)


def _pool_kernel(x_ref, halo_ref, pre_g_ref, post_g_ref, pw_ref, scale_ref, o_ref, hext_ref, m_ref):
    i = pl.program_id(1)
    tm, d = x_ref.shape[1], x_ref.shape[2]
    gc = d // len(POOL_WINDOWS)
    pre_g = pre_g_ref[...]
    h = _rms(x_ref[0], pre_g)
    keep = (i > 0).astype(F32)
    hext_ref[0:HALO, :] = _rms(halo_ref[0], pre_g) * keep
    hext_ref[HALO:HALO + tm, :] = h
    pos = i * tm + lax.broadcasted_iota(jnp.int32, (tm, 1), 0)
    ss = jnp.zeros((tm, 1), F32)
    for g, w in enumerate(POOL_WINDOWS):
        lo, hi = g * gc, (g + 1) * gc
        cur = hext_ref[HALO:HALO + tm, lo:hi]
        acc = cur
        for s in range(1, w):
            acc = acc + hext_ref[HALO - s:HALO - s + tm, lo:hi]
        count = jnp.minimum(pos + 1, w).astype(F32)
        y = acc * (1.0 / count) - cur
        m = jnp.dot(y.astype(BF16), pw_ref[g], preferred_element_type=F32) * scale_ref[:, lo:hi]
        m_ref[:, lo:hi] = m
        ss = ss + jnp.sum(m * m, axis=-1, keepdims=True)
    inv = lax.rsqrt(ss * (1.0 / d) + EPS)
    o_ref[0] = x_ref[0] + m_ref[...] * inv * post_g_ref[...]


def _pool_block(x, pre_g, post_g, pw, scale, *, tm=512):
    b, s, d = x.shape
    g, gc = pw.shape[0], pw.shape[1]
    per_tile = tm // HALO
    est = 2 * 2 * tm * d * 4 + 2 * HALO * d * 4 + g * gc * gc * 2 + (2 * tm + HALO) * d * 4
    return pl.pallas_call(
        _pool_kernel,
        grid=(b, s // tm),
        in_specs=[
            pl.BlockSpec((1, tm, d), lambda bi, i: (bi, i, 0)),
            pl.BlockSpec((1, HALO, d), lambda bi, i: (bi, jnp.maximum(i * per_tile - 1, 0), 0)),
            _const_spec((1, d)),
            _const_spec((1, d)),
            _const_spec((g, gc, gc)),
            _const_spec((1, d)),
        ],
        out_specs=pl.BlockSpec((1, tm, d), lambda bi, i: (bi, i, 0)),
        out_shape=jax.ShapeDtypeStruct((b, s, d), F32),
        scratch_shapes=[pltpu.VMEM((HALO + tm, d), F32), pltpu.VMEM((tm, d), F32)],
        compiler_params=pltpu.CompilerParams(
            dimension_semantics=("arbitrary", "arbitrary"), vmem_limit_bytes=_vmem_limit(est)),
        name="pool_block",
    )(x, x, pre_g, post_g, pw, scale)


def _proj_kernel(x_ref, g_ref, w_ref, *o_refs, out_scale):
    h = _rms(x_ref[...], g_ref[...]).astype(BF16)
    n = w_ref.shape[1] // len(o_refs)
    for j, o_ref in enumerate(o_refs):
        y = jnp.dot(h, w_ref[:, j * n:(j + 1) * n], preferred_element_type=F32)
        if out_scale != 1.0:
            y = y * out_scale
        o_ref[...] = y.astype(o_ref.dtype)


def _norm_proj(x2, g, w, n_out, out_scale=1.0, *, tm=512):
    t, d = x2.shape
    n = w.shape[1] // n_out
    est = 2 * tm * d * 4 + d * w.shape[1] * 2 + 2 * n_out * tm * n * 2
    outs = pl.pallas_call(
        functools.partial(_proj_kernel, out_scale=out_scale),
        grid=(t // tm,),
        in_specs=[
            pl.BlockSpec((tm, d), lambda i: (i, 0)),
            _const_spec((1, d)),
            _const_spec(w.shape),
        ],
        out_specs=[pl.BlockSpec((tm, n), lambda i: (i, 0)) for _ in range(n_out)],
        out_shape=[jax.ShapeDtypeStruct((t, n), BF16) for _ in range(n_out)],
        compiler_params=pltpu.CompilerParams(
            dimension_semantics=("arbitrary",), vmem_limit_bytes=_vmem_limit(est)),
        name="norm_proj",
    )(x2, g, w)
    return outs


def _oproj_kernel(x_ref, a_ref, w_ref, post_g_ref, o_ref):
    m = jnp.dot(a_ref[...], w_ref[...], preferred_element_type=F32)
    o_ref[...] = x_ref[...] + _rms(m, post_g_ref[...])


def _oproj_block(x2, a2, w_o, post_g, *, tm=512):
    t, d = x2.shape
    hd = a2.shape[1]
    est = 2 * 2 * tm * d * 4 + 2 * tm * hd * 2 + hd * d * 2 + tm * d * 4
    return pl.pallas_call(
        _oproj_kernel,
        grid=(t // tm,),
        in_specs=[
            pl.BlockSpec((tm, d), lambda i: (i, 0)),
            pl.BlockSpec((tm, hd), lambda i: (i, 0)),
            _const_spec((hd, d)),
            _const_spec((1, d)),
        ],
        out_specs=pl.BlockSpec((tm, d), lambda i: (i, 0)),
        out_shape=jax.ShapeDtypeStruct((t, d), F32),
        compiler_params=pltpu.CompilerParams(
            dimension_semantics=("arbitrary",), vmem_limit_bytes=_vmem_limit(est)),
        name="oproj_block",
    )(x2, a2, w_o, post_g)


def _softplus(z):
    return jnp.maximum(z, 0.0) + jnp.log(1.0 + jnp.exp(-jnp.abs(z)))


def _attn_kernel(q_ref, k_ref, v_ref, cum_ref, o_ref, acc_ref, carry_ref, *, tq, dh):
    i = pl.program_id(2)
    q2 = q_ref[0]
    lane = lax.broadcasted_iota(jnp.int32, q2.shape, 1)
    zero = jnp.zeros_like(q2)
    q_heads = (jnp.where(lane < dh, q2, zero), jnp.where(lane >= dh, q2, zero))
    acc_ref[...] = jnp.zeros_like(acc_ref)
    carry_ref[...] = jnp.zeros_like(carry_ref)
    row = lax.broadcasted_iota(jnp.int32, (tq, tq), 0)
    col = lax.broadcasted_iota(jnp.int32, (tq, tq), 1)
    causal = col < row

    def visit(j, diagonal):
        start = pl.multiple_of(j * tq, tq)
        kb = k_ref[0, pl.ds(start, tq), :]
        vb = v_ref[0, pl.ds(start, tq), :]
        for h in range(2):
            z = lax.dot_general(q_heads[h], kb, (((1,), (1,)), ((), ())),
                                preferred_element_type=F32)
            sp = _softplus(z)
            if diagonal:
                sp = jnp.where(causal, sp, 0.0)
            hi = sp.astype(BF16)
            lo = (sp - hi.astype(F32)).astype(BF16)
            cum = (jnp.dot(hi, cum_ref[...], preferred_element_type=F32)
                   + jnp.dot(lo, cum_ref[...], preferred_element_type=F32))
            carry = carry_ref[h]
            arg = z + cum[:, :tq] + jnp.concatenate([carry] * (tq // LANES), axis=1)
            if diagonal:
                arg = jnp.where(causal, arg, NEG_BIG)
            a = jnp.exp(arg).astype(BF16)
            acc_ref[h] += jnp.dot(a, vb, preferred_element_type=F32)
            carry_ref[h] = carry + cum[:, tq:]

    visit(i, True)

    def body(n, c):
        visit(i - 1 - n, False)
        return c

    lax.fori_loop(0, i, body, 0)
    out_lane = lax.broadcasted_iota(jnp.int32, (tq, LANES), 1)
    o_ref[0] = jnp.where(out_lane < dh, acc_ref[0], acc_ref[1]).astype(o_ref.dtype)


def _attention(q, k, v, *, tq=256):
    b, s, hd = q.shape
    dh = hd // N_HEADS
    assert 2 * dh == LANES
    jj = lax.broadcasted_iota(jnp.int32, (tq, tq + LANES), 0)
    ss = lax.broadcasted_iota(jnp.int32, (tq, tq + LANES), 1)
    cum_mat = jnp.where((jj >= ss) | (ss >= tq), -1.0, 0.0).astype(BF16)
    est = 2 * 2 * tq * LANES * 2 + 2 * 2 * s * LANES * 2 + tq * (tq + LANES) * 2 + 4 * tq * LANES * 4
    return pl.pallas_call(
        functools.partial(_attn_kernel, tq=tq, dh=dh),
        grid=(b, hd // LANES, s // tq),
        in_specs=[
            pl.BlockSpec((1, tq, LANES), lambda bi, hp, i: (bi, i, hp)),
            pl.BlockSpec((1, s, LANES), lambda bi, hp, i: (bi, 0, hp)),
            pl.BlockSpec((1, s, LANES), lambda bi, hp, i: (bi, 0, hp)),
            _const_spec((tq, tq + LANES)),
        ],
        out_specs=pl.BlockSpec((1, tq, LANES), lambda bi, hp, i: (bi, i, hp)),
        out_shape=jax.ShapeDtypeStruct((b, s, hd), BF16),
        scratch_shapes=[pltpu.VMEM((2, tq, LANES), F32), pltpu.VMEM((2, tq, LANES), F32)],
        compiler_params=pltpu.CompilerParams(
            dimension_semantics=("arbitrary", "arbitrary", "arbitrary"),
            vmem_limit_bytes=_vmem_limit(est)),
        name="stickbreak_attn",
    )(q, k, v, cum_mat)


def kernel(x, pool_w, pool_scale, w_q, w_kv, kv_norm_g, w_o, w_up, w_down,
           mix_pre_g, mix_post_g, mlp_pre_g, mlp_post_g):
    b, s, d = x.shape
    depth = w_up.shape[0]
    n_a = pool_w.shape[0]
    hd = w_q.shape[2]
    inv_sqrt_dh = float((hd // N_HEADS) ** -0.5)
    row = lambda g: g.reshape(1, d).astype(F32)
    k = v = None
    for layer in range(depth):
        if layer < n_a:
            x = _pool_block(x, row(mix_pre_g[layer]), row(mix_post_g[layer]),
                            pool_w[layer].astype(BF16), row(pool_scale[layer]))
        else:
            j = layer - n_a
            (q,) = _norm_proj(x.reshape(b * s, d), row(mix_pre_g[layer]), w_q[j].astype(BF16), 1,
                              out_scale=inv_sqrt_dh)
            o = _attention(q.reshape(b, s, hd), k, v)
            x = _oproj_block(x.reshape(b * s, d), o.reshape(b * s, hd), w_o[j].astype(BF16),
                             row(mix_post_g[layer])).reshape(b, s, d)
        x = _mlp_block(x.reshape(b * s, d), row(mlp_pre_g[layer]), row(mlp_post_g[layer]),
                       w_up[layer].astype(BF16), w_down[layer].astype(BF16)).reshape(b, s, d)
        if layer == n_a - 1:
            k, v = _norm_proj(x.reshape(b * s, d), row(kv_norm_g), w_kv.astype(BF16), 2)
            k = k.reshape(b, s, hd)
            v = v.reshape(b, s, hd)
    return x
```

```python
import functools

import jax
import jax.numpy as jnp
from jax import lax
from jax.experimental import pallas as pl
from jax.experimental.pallas import tpu as pltpu

F32 = jnp.float32
BF16 = jnp.bfloat16

EPS = 1e-6
POOL_WINDOWS = (2, 4, 8, 16)
N_HEADS = 16
HALO = 16
LANES = 128
V7X_VMEM_BYTES = 64 * 1024 * 1024
NEG_BIG = -1e30
DEAD_LOG = -104.0


def _vmem_limit(est_bytes):
    return int(min(V7X_VMEM_BYTES - 4 * 1024 * 1024, est_bytes * 3 // 2 + 8 * 1024 * 1024))


def _rms(x, g):
    ms = jnp.mean(x * x, axis=-1, keepdims=True)
    return x * lax.rsqrt(ms + EPS) * g


def _const_spec(shape):
    nd = len(shape)
    return pl.BlockSpec(shape, lambda *_: (0,) * nd, pipeline_mode=pl.Buffered(1))


def _mlp_kernel(x_ref, pre_g_ref, post_g_ref, wup_ref, wdn_ref, o_ref, a_ref, y_ref, *, fc, nc):
    d = x_ref.shape[-1]
    d_ff = wup_ref.shape[-1]
    h = _rms(x_ref[...], pre_g_ref[...]).astype(BF16)
    for c in range(d_ff // fc):
        u = jnp.dot(h, wup_ref[:, c * fc:(c + 1) * fc], preferred_element_type=F32)
        r = jnp.maximum(u, 0.0)
        a_ref[:, c * fc:(c + 1) * fc] = (r * r).astype(BF16)
    ss = jnp.zeros((x_ref.shape[0], 1), F32)
    for n in range(d // nc):
        y = jnp.dot(a_ref[...], wdn_ref[:, n * nc:(n + 1) * nc], preferred_element_type=F32)
        y_ref[:, n * nc:(n + 1) * nc] = y
        ss = ss + jnp.sum(y * y, axis=-1, keepdims=True)
    inv = lax.rsqrt(ss * (1.0 / d) + EPS)
    o_ref[...] = x_ref[...] + y_ref[...] * inv * post_g_ref[...]


def _mlp_block(x2, pre_g, post_g, w_up, w_dn, *, tm=512, fc=512, nc=256):
    t, d = x2.shape
    d_ff = w_up.shape[1]
    est = (2 * 2 * tm * d * 4 + 2 * d * d_ff * 2 + tm * d_ff * 2 + tm * d * 4)
    return pl.pallas_call(
        functools.partial(_mlp_kernel, fc=fc, nc=nc),
        grid=(t // tm,),
        in_specs=[
            pl.BlockSpec((tm, d), lambda i: (i, 0)),
            _const_spec((1, d)),
            _const_spec((1, d)),
            _const_spec((d, d_ff)),
            _const_spec((d_ff, d)),
        ],
        out_specs=pl.BlockSpec((tm, d), lambda i: (i, 0)),
        out_shape=jax.ShapeDtypeStruct((t, d), F32),
        scratch_shapes=[pltpu.VMEM((tm, d_ff), BF16), pltpu.VMEM((tm, d), F32)],
        compiler_params=pltpu.CompilerParams(
            dimension_semantics=("arbitrary",), vmem_limit_bytes=_vmem_limit(est)),
        name="mlp_block",
    )(x2, pre_g, post_g, w_up, w_dn)


def _pool_kernel(x_ref, halo_ref, pre_g_ref, post_g_ref, pw_ref, scale_ref, o_ref, hext_ref, m_ref):
    i = pl.program_id(1)
    tm, d = x_ref.shape[1], x_ref.shape[2]
    gc = d // len(POOL_WINDOWS)
    pre_g = pre_g_ref[...]
    h = _rms(x_ref[0], pre_g)
    keep = (i > 0).astype(F32)
    hext_ref[0:HALO, :] = _rms(halo_ref[0], pre_g) * keep
    hext_ref[HALO:HALO + tm, :] = h
    pos = i * tm + lax.broadcasted_iota(jnp.int32, (tm, 1), 0)
    ss = jnp.zeros((tm, 1), F32)
    for g, w in enumerate(POOL_WINDOWS):
        lo, hi = g * gc, (g + 1) * gc
        cur = hext_ref[HALO:HALO + tm, lo:hi]
        acc = cur
        for s in range(1, w):
            acc = acc + hext_ref[HALO - s:HALO - s + tm, lo:hi]
        count = jnp.minimum(pos + 1, w).astype(F32)
        y = acc * (1.0 / count) - cur
        m = jnp.dot(y.astype(BF16), pw_ref[g], preferred_element_type=F32) * scale_ref[:, lo:hi]
        m_ref[:, lo:hi] = m
        ss = ss + jnp.sum(m * m, axis=-1, keepdims=True)
    inv = lax.rsqrt(ss * (1.0 / d) + EPS)
    o_ref[0] = x_ref[0] + m_ref[...] * inv * post_g_ref[...]


def _pool_block(x, pre_g, post_g, pw, scale, *, tm=512):
    b, s, d = x.shape
    g, gc = pw.shape[0], pw.shape[1]
    per_tile = tm // HALO
    est = 2 * 2 * tm * d * 4 + 2 * HALO * d * 4 + g * gc * gc * 2 + (2 * tm + HALO) * d * 4
    return pl.pallas_call(
        _pool_kernel,
        grid=(b, s // tm),
        in_specs=[
            pl.BlockSpec((1, tm, d), lambda bi, i: (bi, i, 0)),
            pl.BlockSpec((1, HALO, d), lambda bi, i: (bi, jnp.maximum(i * per_tile - 1, 0), 0)),
            _const_spec((1, d)),
            _const_spec((1, d)),
            _const_spec((g, gc, gc)),
            _const_spec((1, d)),
        ],
        out_specs=pl.BlockSpec((1, tm, d), lambda bi, i: (bi, i, 0)),
        out_shape=jax.ShapeDtypeStruct((b, s, d), F32),
        scratch_shapes=[pltpu.VMEM((HALO + tm, d), F32), pltpu.VMEM((tm, d), F32)],
        compiler_params=pltpu.CompilerParams(
            dimension_semantics=("arbitrary", "arbitrary"), vmem_limit_bytes=_vmem_limit(est)),
        name="pool_block",
    )(x, x, pre_g, post_g, pw, scale)


def _proj_kernel(x_ref, g_ref, w_ref, *o_refs, out_scale):
    h = _rms(x_ref[...], g_ref[...]).astype(BF16)
    n = w_ref.shape[1] // len(o_refs)
    for j, o_ref in enumerate(o_refs):
        y = jnp.dot(h, w_ref[:, j * n:(j + 1) * n], preferred_element_type=F32)
        if out_scale != 1.0:
            y = y * out_scale
        o_ref[...] = y.astype(o_ref.dtype)


def _norm_proj(x2, g, w, n_out, out_scale=1.0, *, tm=512):
    t, d = x2.shape
    n = w.shape[1] // n_out
    est = 2 * tm * d * 4 + d * w.shape[1] * 2 + 2 * n_out * tm * n * 2
    outs = pl.pallas_call(
        functools.partial(_proj_kernel, out_scale=out_scale),
        grid=(t // tm,),
        in_specs=[
            pl.BlockSpec((tm, d), lambda i: (i, 0)),
            _const_spec((1, d)),
            _const_spec(w.shape),
        ],
        out_specs=[pl.BlockSpec((tm, n), lambda i: (i, 0)) for _ in range(n_out)],
        out_shape=[jax.ShapeDtypeStruct((t, n), BF16) for _ in range(n_out)],
        compiler_params=pltpu.CompilerParams(
            dimension_semantics=("arbitrary",), vmem_limit_bytes=_vmem_limit(est)),
        name="norm_proj",
    )(x2, g, w)
    return outs


def _oproj_kernel(x_ref, a_ref, w_ref, post_g_ref, o_ref):
    m = jnp.dot(a_ref[...], w_ref[...], preferred_element_type=F32)
    o_ref[...] = x_ref[...] + _rms(m, post_g_ref[...])


def _oproj_block(x2, a2, w_o, post_g, *, tm=512):
    t, d = x2.shape
    hd = a2.shape[1]
    est = 2 * 2 * tm * d * 4 + 2 * tm * hd * 2 + hd * d * 2 + tm * d * 4
    return pl.pallas_call(
        _oproj_kernel,
        grid=(t // tm,),
        in_specs=[
            pl.BlockSpec((tm, d), lambda i: (i, 0)),
            pl.BlockSpec((tm, hd), lambda i: (i, 0)),
            _const_spec((hd, d)),
            _const_spec((1, d)),
        ],
        out_specs=pl.BlockSpec((tm, d), lambda i: (i, 0)),
        out_shape=jax.ShapeDtypeStruct((t, d), F32),
        compiler_params=pltpu.CompilerParams(
            dimension_semantics=("arbitrary",), vmem_limit_bytes=_vmem_limit(est)),
        name="oproj_block",
    )(x2, a2, w_o, post_g)


def _softplus(z):
    return jnp.maximum(z, 0.0) + jnp.log(1.0 + jnp.exp(-jnp.abs(z)))


def _attn_kernel(q_ref, k_ref, v_ref, cum_ref, o_ref, acc_ref, carry_ref, *, tq, dh, window):
    s, width = q_ref.shape[1], q_ref.shape[2]
    n_tiles = s // tq
    n_pairs = width // LANES
    row = lax.broadcasted_iota(jnp.int32, (tq, tq), 0)
    col = lax.broadcasted_iota(jnp.int32, (tq, tq), 1)
    causal = col < row
    lane = lax.broadcasted_iota(jnp.int32, (tq, LANES), 1)
    head_lanes = (lane < dh, lane >= dh)

    def head_block(qh, kb, carry, diagonal):
        z = lax.dot_general(qh, kb, (((1,), (1,)), ((), ())), preferred_element_type=F32)
        sp = _softplus(z)
        if diagonal:
            sp = jnp.where(causal, sp, 0.0)
        cum = jnp.dot(sp.astype(BF16), cum_ref[...], preferred_element_type=F32)
        arg = z + cum if carry is None else z + cum + carry
        if diagonal:
            arg = jnp.where(causal, arg, NEG_BIG)
        total = jnp.sum(sp, axis=1, keepdims=True)
        return arg, (-total if carry is None else carry - total)

    def tile(i, n_prev):
        q0 = pl.multiple_of(i * tq, tq)
        k0 = pl.multiple_of((i - n_prev) * tq, tq)
        nk = (n_prev + 1) * tq
        live = None
        state = []
        for p in range(n_pairs):
            cols = slice(p * LANES, (p + 1) * LANES)
            qp = q_ref[0, pl.ds(q0, tq), cols]
            kw = k_ref[0, pl.ds(k0, nk), cols]
            vw = v_ref[0, pl.ds(k0, nk), cols]
            outs = []
            for hh in range(2):
                qh = jnp.where(head_lanes[hh], qp, jnp.zeros_like(qp))
                carry = None
                args = [None] * (n_prev + 1)
                for b in range(n_prev, -1, -1):
                    args[b], carry = head_block(qh, kw[b * tq:(b + 1) * tq], carry, b == n_prev)
                a = jnp.exp(jnp.concatenate(args, axis=1)).astype(BF16)
                outs.append(jnp.dot(a, vw, preferred_element_type=F32))
                state.append(carry)
                live = carry if live is None else jnp.maximum(live, carry)
            state.append(outs)
            o_ref[0, pl.ds(q0, tq), cols] = jnp.where(head_lanes[0], outs[0], outs[1]).astype(o_ref.dtype)
        return live, state

    def finish(i, first_old, live, state):
        def run():
            for p in range(n_pairs):
                for hh in range(2):
                    acc_ref[2 * p + hh] = state[3 * p + 2][hh]
                    carry_ref[2 * p + hh] = jnp.broadcast_to(state[3 * p + hh], (tq, LANES))

            def cond(st):
                return jnp.logical_and(st[0] >= 0, st[1])

            def body(st):
                j = st[0]
                k0 = pl.multiple_of(j * tq, tq)
                q0 = pl.multiple_of(i * tq, tq)
                mx = None
                for p in range(n_pairs):
                    cols = slice(p * LANES, (p + 1) * LANES)
                    qp = q_ref[0, pl.ds(q0, tq), cols]
                    kb = k_ref[0, pl.ds(k0, tq), cols]
                    vb = v_ref[0, pl.ds(k0, tq), cols]
                    for hh in range(2):
                        h = 2 * p + hh
                        qh = jnp.where(head_lanes[hh], qp, jnp.zeros_like(qp))
                        arg, carry = head_block(qh, kb, carry_ref[h], False)
                        acc_ref[h] += jnp.dot(jnp.exp(arg).astype(BF16), vb, preferred_element_type=F32)
                        carry_ref[h] = carry
                        mx = carry if mx is None else jnp.maximum(mx, carry)
                return j - 1, jnp.max(mx) > DEAD_LOG

            lax.while_loop(cond, body, (first_old, jnp.bool_(True)))
            q0 = pl.multiple_of(i * tq, tq)
            for p in range(n_pairs):
                cols = slice(p * LANES, (p + 1) * LANES)
                o_ref[0, pl.ds(q0, tq), cols] = jnp.where(
                    head_lanes[0], acc_ref[2 * p], acc_ref[2 * p + 1]).astype(o_ref.dtype)

        pl.when(jnp.logical_and(first_old >= 0, jnp.max(live) > DEAD_LOG))(run)

    for i in range(min(window, n_tiles)):
        tile(i, i)

    def loop_body(i, c):
        live, state = tile(i, window)
        finish(i, i - window - 1, live, state)
        return c

    lax.fori_loop(window, n_tiles, loop_body, 0)


def _attention(q, k, v, *, tq=128, window=2, heads_per_step=4):
    b, s, hd = q.shape
    dh = hd // N_HEADS
    assert 2 * dh == LANES and tq == LANES
    width = heads_per_step * dh
    jj = lax.broadcasted_iota(jnp.int32, (tq, tq), 0)
    ss = lax.broadcasted_iota(jnp.int32, (tq, tq), 1)
    cum_mat = jnp.where(jj >= ss, -1.0, 0.0).astype(BF16)
    est = 4 * 2 * s * width * 2 + tq * tq * 2 + 2 * heads_per_step * tq * LANES * 4
    seq_spec = pl.BlockSpec((1, s, width), lambda bi, g: (bi, 0, g))
    return pl.pallas_call(
        functools.partial(_attn_kernel, tq=tq, dh=dh, window=window),
        grid=(b, hd // width),
        in_specs=[seq_spec, seq_spec, seq_spec, _const_spec((tq, tq))],
        out_specs=seq_spec,
        out_shape=jax.ShapeDtypeStruct((b, s, hd), BF16),
        scratch_shapes=[pltpu.VMEM((heads_per_step, tq, LANES), F32),
                        pltpu.VMEM((heads_per_step, tq, LANES), F32)],
        compiler_params=pltpu.CompilerParams(
            dimension_semantics=("arbitrary", "arbitrary"), vmem_limit_bytes=_vmem_limit(est)),
        name="stickbreak_attn",
    )(q, k, v, cum_mat)


def kernel(x, pool_w, pool_scale, w_q, w_kv, kv_norm_g, w_o, w_up, w_down,
           mix_pre_g, mix_post_g, mlp_pre_g, mlp_post_g):
    b, s, d = x.shape
    depth = w_up.shape[0]
    n_a = pool_w.shape[0]
    hd = w_q.shape[2]
    inv_sqrt_dh = float((hd // N_HEADS) ** -0.5)
    row = lambda g: g.reshape(1, d).astype(F32)
    k = v = None
    for layer in range(depth):
        if layer < n_a:
            x = _pool_block(x, row(mix_pre_g[layer]), row(mix_post_g[layer]),
                            pool_w[layer].astype(BF16), row(pool_scale[layer]))
        else:
            j = layer - n_a
            (q,) = _norm_proj(x.reshape(b * s, d), row(mix_pre_g[layer]), w_q[j].astype(BF16), 1,
                              out_scale=inv_sqrt_dh)
            o = _attention(q.reshape(b, s, hd), k, v)
            x = _oproj_block(x.reshape(b * s, d), o.reshape(b * s, hd), w_o[j].astype(BF16),
                             row(mix_post_g[layer])).reshape(b, s, d)
        x = _mlp_block(x.reshape(b * s, d), row(mlp_pre_g[layer]), row(mlp_post_g[layer]),
                       w_up[layer].astype(BF16), w_down[layer].astype(BF16)).reshape(b, s, d)
        if layer == n_a - 1:
            k, v = _norm_proj(x.reshape(b * s, d), row(kv_norm_g), w_kv.astype(BF16), 2)
            k = k.reshape(b, s, hd)
            v = v.reshape(b, s, hd)
    return x
```

```python
import functools

import jax
import jax.numpy as jnp
from jax import lax
from jax.experimental import pallas as pl
from jax.experimental.pallas import tpu as pltpu

F32 = jnp.float32
BF16 = jnp.bfloat16

EPS = 1e-6
POOL_WINDOWS = (2, 4, 8, 16)
N_HEADS = 16
HALO = 16
LANES = 128
V7X_VMEM_BYTES = 64 * 1024 * 1024
NEG_BIG = -1e30
DEAD_LOG2 = -150.0
LOG2E = 1.4426950408889634


def _vmem_limit(est_bytes):
    return int(min(V7X_VMEM_BYTES - 4 * 1024 * 1024, est_bytes * 3 // 2 + 8 * 1024 * 1024))


def _rms(x, g):
    ms = jnp.mean(x * x, axis=-1, keepdims=True)
    return x * lax.rsqrt(ms + EPS) * g


def _const_spec(shape):
    nd = len(shape)
    return pl.BlockSpec(shape, lambda *_: (0,) * nd, pipeline_mode=pl.Buffered(1))


def _mlp_kernel(x_ref, pre_g_ref, post_g_ref, wup_ref, wdn_ref, o_ref, a_ref, y_ref, *, fc, nc):
    d = x_ref.shape[-1]
    d_ff = wup_ref.shape[-1]
    h = _rms(x_ref[...], pre_g_ref[...]).astype(BF16)
    for c in range(d_ff // fc):
        u = jnp.dot(h, wup_ref[:, c * fc:(c + 1) * fc], preferred_element_type=F32)
        r = jnp.maximum(u, 0.0)
        a_ref[:, c * fc:(c + 1) * fc] = (r * r).astype(BF16)
    ss = jnp.zeros((x_ref.shape[0], 1), F32)
    for n in range(d // nc):
        y = jnp.dot(a_ref[...], wdn_ref[:, n * nc:(n + 1) * nc], preferred_element_type=F32)
        y_ref[:, n * nc:(n + 1) * nc] = y
        ss = ss + jnp.sum(y * y, axis=-1, keepdims=True)
    inv = lax.rsqrt(ss * (1.0 / d) + EPS)
    o_ref[...] = x_ref[...] + y_ref[...] * inv * post_g_ref[...]


def _mlp_block(x2, pre_g, post_g, w_up, w_dn, *, tm=512, fc=512, nc=256):
    t, d = x2.shape
    d_ff = w_up.shape[1]
    est = (2 * 2 * tm * d * 4 + 2 * d * d_ff * 2 + tm * d_ff * 2 + tm * d * 4)
    return pl.pallas_call(
        functools.partial(_mlp_kernel, fc=fc, nc=nc),
        grid=(t // tm,),
        in_specs=[
            pl.BlockSpec((tm, d), lambda i: (i, 0)),
            _const_spec((1, d)),
            _const_spec((1, d)),
            _const_spec((d, d_ff)),
            _const_spec((d_ff, d)),
        ],
        out_specs=pl.BlockSpec((tm, d), lambda i: (i, 0)),
        out_shape=jax.ShapeDtypeStruct((t, d), F32),
        scratch_shapes=[pltpu.VMEM((tm, d_ff), BF16), pltpu.VMEM((tm, d), F32)],
        compiler_params=pltpu.CompilerParams(
            dimension_semantics=("arbitrary",), vmem_limit_bytes=_vmem_limit(est)),
        name="mlp_block",
    )(x2, pre_g, post_g, w_up, w_dn)


def _pool_kernel(x_ref, halo_ref, pre_g_ref, post_g_ref, pw_ref, scale_ref, o_ref, hext_ref, m_ref):
    i = pl.program_id(1)
    tm, d = x_ref.shape[1], x_ref.shape[2]
    gc = d // len(POOL_WINDOWS)
    pre_g = pre_g_ref[...]
    h = _rms(x_ref[0], pre_g)
    keep = (i > 0).astype(F32)
    hext_ref[0:HALO, :] = _rms(halo_ref[0], pre_g) * keep
    hext_ref[HALO:HALO + tm, :] = h
    pos = i * tm + lax.broadcasted_iota(jnp.int32, (tm, 1), 0)
    ss = jnp.zeros((tm, 1), F32)
    for g, w in enumerate(POOL_WINDOWS):
        lo, hi = g * gc, (g + 1) * gc
        cur = hext_ref[HALO:HALO + tm, lo:hi]
        acc = cur
        for s in range(1, w):
            acc = acc + hext_ref[HALO - s:HALO - s + tm, lo:hi]
        count = jnp.minimum(pos + 1, w).astype(F32)
        y = acc * (1.0 / count) - cur
        m = jnp.dot(y.astype(BF16), pw_ref[g], preferred_element_type=F32) * scale_ref[:, lo:hi]
        m_ref[:, lo:hi] = m
        ss = ss + jnp.sum(m * m, axis=-1, keepdims=True)
    inv = lax.rsqrt(ss * (1.0 / d) + EPS)
    o_ref[0] = x_ref[0] + m_ref[...] * inv * post_g_ref[...]


def _pool_block(x, pre_g, post_g, pw, scale, *, tm=512):
    b, s, d = x.shape
    g, gc = pw.shape[0], pw.shape[1]
    per_tile = tm // HALO
    est = 2 * 2 * tm * d * 4 + 2 * HALO * d * 4 + g * gc * gc * 2 + (2 * tm + HALO) * d * 4
    return pl.pallas_call(
        _pool_kernel,
        grid=(b, s // tm),
        in_specs=[
            pl.BlockSpec((1, tm, d), lambda bi, i: (bi, i, 0)),
            pl.BlockSpec((1, HALO, d), lambda bi, i: (bi, jnp.maximum(i * per_tile - 1, 0), 0)),
            _const_spec((1, d)),
            _const_spec((1, d)),
            _const_spec((g, gc, gc)),
            _const_spec((1, d)),
        ],
        out_specs=pl.BlockSpec((1, tm, d), lambda bi, i: (bi, i, 0)),
        out_shape=jax.ShapeDtypeStruct((b, s, d), F32),
        scratch_shapes=[pltpu.VMEM((HALO + tm, d), F32), pltpu.VMEM((tm, d), F32)],
        compiler_params=pltpu.CompilerParams(
            dimension_semantics=("arbitrary", "arbitrary"), vmem_limit_bytes=_vmem_limit(est)),
        name="pool_block",
    )(x, x, pre_g, post_g, pw, scale)


def _proj_kernel(x_ref, g_ref, w_ref, *o_refs, out_scale):
    h = _rms(x_ref[...], g_ref[...]).astype(BF16)
    n = w_ref.shape[1] // len(o_refs)
    for j, o_ref in enumerate(o_refs):
        y = jnp.dot(h, w_ref[:, j * n:(j + 1) * n], preferred_element_type=F32)
        if out_scale != 1.0:
            y = y * out_scale
        o_ref[...] = y.astype(o_ref.dtype)


def _norm_proj(x2, g, w, n_out, out_scale=1.0, *, tm=512):
    t, d = x2.shape
    n = w.shape[1] // n_out
    est = 2 * tm * d * 4 + d * w.shape[1] * 2 + 2 * n_out * tm * n * 2
    outs = pl.pallas_call(
        functools.partial(_proj_kernel, out_scale=out_scale),
        grid=(t // tm,),
        in_specs=[
            pl.BlockSpec((tm, d), lambda i: (i, 0)),
            _const_spec((1, d)),
            _const_spec(w.shape),
        ],
        out_specs=[pl.BlockSpec((tm, n), lambda i: (i, 0)) for _ in range(n_out)],
        out_shape=[jax.ShapeDtypeStruct((t, n), BF16) for _ in range(n_out)],
        compiler_params=pltpu.CompilerParams(
            dimension_semantics=("arbitrary",), vmem_limit_bytes=_vmem_limit(est)),
        name="norm_proj",
    )(x2, g, w)
    return outs


def _oproj_kernel(x_ref, a_ref, w_ref, post_g_ref, o_ref):
    m = jnp.dot(a_ref[...], w_ref[...], preferred_element_type=F32)
    o_ref[...] = x_ref[...] + _rms(m, post_g_ref[...])


def _oproj_block(x2, a2, w_o, post_g, *, tm=512):
    t, d = x2.shape
    hd = a2.shape[1]
    est = 2 * 2 * tm * d * 4 + 2 * tm * hd * 2 + hd * d * 2 + tm * d * 4
    return pl.pallas_call(
        _oproj_kernel,
        grid=(t // tm,),
        in_specs=[
            pl.BlockSpec((tm, d), lambda i: (i, 0)),
            pl.BlockSpec((tm, hd), lambda i: (i, 0)),
            _const_spec((hd, d)),
            _const_spec((1, d)),
        ],
        out_specs=pl.BlockSpec((tm, d), lambda i: (i, 0)),
        out_shape=jax.ShapeDtypeStruct((t, d), F32),
        compiler_params=pltpu.CompilerParams(
            dimension_semantics=("arbitrary",), vmem_limit_bytes=_vmem_limit(est)),
        name="oproj_block",
    )(x2, a2, w_o, post_g)


def _softplus2(y):
    neg_abs = lax.bitcast_convert_type(
        lax.bitcast_convert_type(y, jnp.uint32) | jnp.uint32(0x80000000), F32)
    return jnp.maximum(y, 0.0) + jnp.log(1.0 + jnp.exp2(neg_abs)) * LOG2E


def _attn_kernel(q_ref, k_ref, v_ref, cum_ref, o_ref, acc_ref, carry_ref, sp_ref, z_ref, a_ref,
                 *, tq, dh, window):
    s, width = q_ref.shape[1], q_ref.shape[2]
    n_tiles = s // tq
    n_pairs = width // LANES
    row = lax.broadcasted_iota(jnp.int32, (tq, tq), 0)
    col = lax.broadcasted_iota(jnp.int32, (tq, tq), 1)
    causal = col < row
    lane = lax.broadcasted_iota(jnp.int32, (tq, LANES), 1)
    head_lanes = (lane < dh, lane >= dh)

    def head_block(qh, kb, carry, diagonal):
        z = lax.dot_general(qh, kb, (((1,), (1,)), ((), ())), preferred_element_type=F32)
        sp = _softplus2(z)
        if diagonal:
            sp = jnp.where(causal, sp, 0.0)
        cum = jnp.dot(sp.astype(BF16), cum_ref[0:tq, 0:tq], preferred_element_type=F32)
        arg = z + cum if carry is None else z + cum + carry
        if diagonal:
            arg = jnp.where(causal, arg, NEG_BIG)
        total = jnp.sum(sp, axis=1, keepdims=True)
        return arg, (-total if carry is None else carry - total)

    def scores(i, n_prev):
        q0 = pl.multiple_of(i * tq, tq)
        k0 = pl.multiple_of((i - n_prev) * tq, tq)
        nb = n_prev + 1
        live = None
        for p in range(n_pairs):
            cols = slice(p * LANES, (p + 1) * LANES)
            qp = q_ref[0, pl.ds(q0, tq), cols]
            kw = k_ref[0, pl.ds(k0, nb * tq), cols]
            for hh in range(2):
                qh = jnp.where(head_lanes[hh], qp, jnp.zeros_like(qp))
                z = lax.dot_general(qh, kw, (((1,), (1,)), ((), ())), preferred_element_type=F32)
                for b in range(nb):
                    r0 = (b * 2 + hh) * tq
                    z_ref[r0:r0 + tq, cols] = z[:, b * tq:(b + 1) * tq]
        for p in range(n_pairs):
            cols = slice(p * LANES, (p + 1) * LANES)
            for hh in range(2):
                carry = None
                for b in range(n_prev, -1, -1):
                    r0 = (b * 2 + hh) * tq
                    zb = z_ref[r0:r0 + tq, cols]
                    sp = _softplus2(zb)
                    arg = zb if carry is None else zb + carry
                    if b == n_prev:
                        sp = jnp.where(causal, sp, 0.0)
                        arg = jnp.where(causal, arg, NEG_BIG)
                    sp_ref[r0:r0 + tq, cols] = sp.astype(BF16)
                    z_ref[r0:r0 + tq, cols] = arg
                    total = jnp.sum(sp, axis=1, keepdims=True)
                    carry = -total if carry is None else carry - total
                carry_ref[2 * p + hh] = jnp.broadcast_to(carry, (tq, LANES))
                live = carry if live is None else jnp.maximum(live, carry)
        return jnp.max(live) > DEAD_LOG2

    def weights_values(i, n_prev):
        q0 = pl.multiple_of(i * tq, tq)
        k0 = pl.multiple_of((i - n_prev) * tq, tq)
        nb = n_prev + 1
        for b in range(nb):
            blk = slice(b * 2 * tq, (b + 1) * 2 * tq)
            cum = jnp.dot(sp_ref[blk, :], cum_ref[...], preferred_element_type=F32)
            a_ref[blk, :] = jnp.exp2(z_ref[blk, :] + cum).astype(BF16)
        accs = []
        for p in range(n_pairs):
            cols = slice(p * LANES, (p + 1) * LANES)
            acc = None
            for b in range(nb):
                r0 = b * 2 * tq
                vb = v_ref[0, pl.ds(pl.multiple_of(k0 + b * tq, tq), tq), cols]
                part = jnp.dot(a_ref[r0:r0 + 2 * tq, cols], vb, preferred_element_type=F32)
                acc = part if acc is None else acc + part
            accs.append(acc)
            o_ref[0, pl.ds(q0, tq), cols] = jnp.where(head_lanes[0], acc[:tq], acc[tq:]).astype(o_ref.dtype)
        return accs

    def finish(i, first_old, live, accs):
        def run():
            for p in range(n_pairs):
                for hh in range(2):
                    acc_ref[2 * p + hh] = accs[p][hh * tq:(hh + 1) * tq]

            def cond(st):
                return jnp.logical_and(st[0] >= 0, st[1])

            def body(st):
                j = st[0]
                k0 = pl.multiple_of(j * tq, tq)
                q0 = pl.multiple_of(i * tq, tq)
                mx = None
                for p in range(n_pairs):
                    cols = slice(p * LANES, (p + 1) * LANES)
                    qp = q_ref[0, pl.ds(q0, tq), cols]
                    kb = k_ref[0, pl.ds(k0, tq), cols]
                    vb = v_ref[0, pl.ds(k0, tq), cols]
                    for hh in range(2):
                        h = 2 * p + hh
                        qh = jnp.where(head_lanes[hh], qp, jnp.zeros_like(qp))
                        arg, carry = head_block(qh, kb, carry_ref[h], False)
                        acc_ref[h] += jnp.dot(jnp.exp2(arg).astype(BF16), vb, preferred_element_type=F32)
                        carry_ref[h] = carry
                        mx = carry if mx is None else jnp.maximum(mx, carry)
                return j - 1, jnp.max(mx) > DEAD_LOG2

            lax.while_loop(cond, body, (first_old, jnp.bool_(True)))
            q0 = pl.multiple_of(i * tq, tq)
            for p in range(n_pairs):
                cols = slice(p * LANES, (p + 1) * LANES)
                o_ref[0, pl.ds(q0, tq), cols] = jnp.where(
                    head_lanes[0], acc_ref[2 * p], acc_ref[2 * p + 1]).astype(o_ref.dtype)

        pl.when(jnp.logical_and(first_old >= 0, live))(run)

    for i in range(window):
        scores(i, i)
        weights_values(i, i)

    def tile_body(i, c):
        live = scores(i, window)
        accs = weights_values(i, window)
        finish(i, i - window - 1, live, accs)
        return c

    lax.fori_loop(window, n_tiles, tile_body, 0)


def _attention(q, k, v, *, tq=128, window=2, heads_per_step=4):
    b, s, hd = q.shape
    dh = hd // N_HEADS
    assert 2 * dh == LANES and tq == LANES
    assert s // tq > window
    width = heads_per_step * dh
    jj = lax.broadcasted_iota(jnp.int32, (width, width), 0)
    ss = lax.broadcasted_iota(jnp.int32, (width, width), 1)
    cum_mat = jnp.where((jj >= ss) & (jj // tq == ss // tq), -1.0, 0.0).astype(BF16)
    staged_rows = 2 * (window + 1) * tq
    est = (4 * 2 * s * width * 2 + width * width * 2 + 2 * heads_per_step * tq * LANES * 4
           + staged_rows * width * (2 + 4 + 2))
    seq_spec = pl.BlockSpec((1, s, width), lambda bi, g: (bi, 0, g))
    return pl.pallas_call(
        functools.partial(_attn_kernel, tq=tq, dh=dh, window=window),
        grid=(b, hd // width),
        in_specs=[seq_spec, seq_spec, seq_spec, _const_spec((width, width))],
        out_specs=seq_spec,
        out_shape=jax.ShapeDtypeStruct((b, s, hd), BF16),
        scratch_shapes=[pltpu.VMEM((heads_per_step, tq, LANES), F32),
                        pltpu.VMEM((heads_per_step, tq, LANES), F32),
                        pltpu.VMEM((staged_rows, width), BF16),
                        pltpu.VMEM((staged_rows, width), F32),
                        pltpu.VMEM((staged_rows, width), BF16)],
        compiler_params=pltpu.CompilerParams(
            dimension_semantics=("arbitrary", "arbitrary"), vmem_limit_bytes=_vmem_limit(est)),
        name="stickbreak_attn",
    )(q, k, v, cum_mat)


def kernel(x, pool_w, pool_scale, w_q, w_kv, kv_norm_g, w_o, w_up, w_down,
           mix_pre_g, mix_post_g, mlp_pre_g, mlp_post_g):
    b, s, d = x.shape
    depth = w_up.shape[0]
    n_a = pool_w.shape[0]
    hd = w_q.shape[2]
    q_scale = float((hd // N_HEADS) ** -0.5) * LOG2E
    row = lambda g: g.reshape(1, d).astype(F32)
    k = v = None
    for layer in range(depth):
        if layer < n_a:
            x = _pool_block(x, row(mix_pre_g[layer]), row(mix_post_g[layer]),
                            pool_w[layer].astype(BF16), row(pool_scale[layer]))
        else:
            j = layer - n_a
            (q,) = _norm_proj(x.reshape(b * s, d), row(mix_pre_g[layer]), w_q[j].astype(BF16), 1,
                              out_scale=q_scale)
            o = _attention(q.reshape(b, s, hd), k, v)
            x = _oproj_block(x.reshape(b * s, d), o.reshape(b * s, hd), w_o[j].astype(BF16),
                             row(mix_post_g[layer])).reshape(b, s, d)
        x = _mlp_block(x.reshape(b * s, d), row(mlp_pre_g[layer]), row(mlp_post_g[layer]),
                       w_up[layer].astype(BF16), w_down[layer].astype(BF16)).reshape(b, s, d)
        if layer == n_a - 1:
            k, v = _norm_proj(x.reshape(b * s, d), row(kv_norm_g), w_kv.astype(BF16), 2)
            k = k.reshape(b, s, hd)
            v = v.reshape(b, s, hd)
    return x
```

```python
import functools

import jax
import jax.numpy as jnp
from jax import lax
from jax.experimental import pallas as pl
from jax.experimental.pallas import tpu as pltpu

F32 = jnp.float32
BF16 = jnp.bfloat16

EPS = 1e-6
POOL_WINDOWS = (2, 4, 8, 16)
N_HEADS = 16
HALO = 16
LANES = 128
V7X_VMEM_BYTES = 64 * 1024 * 1024
V7X_MXU_DIM = 256
NEG_BIG = -1e30
DEAD_LOG2 = -150.0
LOG2E = 1.4426950408889634


def _vmem_limit(est_bytes):
    return int(min(V7X_VMEM_BYTES - 4 * 1024 * 1024, est_bytes * 3 // 2 + 8 * 1024 * 1024))


def _rms(x, g):
    ms = jnp.mean(x * x, axis=-1, keepdims=True)
    return x * lax.rsqrt(ms + EPS) * g


def _const_spec(shape):
    nd = len(shape)
    return pl.BlockSpec(shape, lambda *_: (0,) * nd, pipeline_mode=pl.Buffered(1))


def _dot_rms_update(lhs_ref, w_ref, g_ref, res_ref, y_ref, dst_ref, nc):
    tm, d = y_ref.shape
    ss = jnp.zeros((tm, 1), F32)
    for n in range(d // nc):
        y = jnp.dot(lhs_ref[...], w_ref[:, n * nc:(n + 1) * nc], preferred_element_type=F32)
        y_ref[:, n * nc:(n + 1) * nc] = y
        ss = ss + jnp.sum(y * y, axis=-1, keepdims=True)
    inv = lax.rsqrt(ss * (1.0 / d) + EPS)
    dst_ref[...] = res_ref[...] + y_ref[...] * inv * g_ref[...]


def _mlp_kernel(*refs, fc, nc, has_attn, proj_outs, proj_scales):
    refs = list(refs)
    x_ref = refs.pop(0)
    if has_attn:
        attn_ref, wo_ref, mix_post_g_ref = refs.pop(0), refs.pop(0), refs.pop(0)
    pre_g_ref, post_g_ref, wup_ref, wdn_ref = (refs.pop(0) for _ in range(4))
    projs = [(refs.pop(0), refs.pop(0)) for _ in proj_outs]
    o_ref = refs.pop(0)
    proj_refs = [[refs.pop(0) for _ in range(n)] for n in proj_outs]
    a_ref, y_ref = refs.pop(0), refs.pop(0)
    d_ff = wup_ref.shape[-1]

    if has_attn:
        x1_ref = refs.pop(0)
        _dot_rms_update(attn_ref, wo_ref, mix_post_g_ref, x_ref, y_ref, x1_ref, nc)
    else:
        x1_ref = x_ref
    h = _rms(x1_ref[...], pre_g_ref[...]).astype(BF16)
    for c in range(d_ff // fc):
        u = jnp.dot(h, wup_ref[:, c * fc:(c + 1) * fc], preferred_element_type=F32)
        r = jnp.maximum(u, 0.0)
        a_ref[:, c * fc:(c + 1) * fc] = (r * r).astype(BF16)
    _dot_rms_update(a_ref, wdn_ref, post_g_ref, x1_ref, y_ref, o_ref, nc)
    for (g_ref, w_ref), outs, scale in zip(projs, proj_refs, proj_scales):
        hn = _rms(o_ref[...], g_ref[...]).astype(BF16)
        width = outs[0].shape[-1]
        for j, p_ref in enumerate(outs):
            for n in range(width // nc):
                c0 = j * width + n * nc
                y = jnp.dot(hn, w_ref[:, c0:c0 + nc], preferred_element_type=F32)
                p_ref[:, n * nc:(n + 1) * nc] = (y * scale if scale != 1.0 else y).astype(p_ref.dtype)


def _mlp_block(x2, pre_g, post_g, w_up, w_dn, *, attn=None, projs=(), tm=512, fc=512, nc=256):
    t, d = x2.shape
    d_ff = w_up.shape[1]
    row_spec = lambda n: pl.BlockSpec((tm, n), lambda i: (i, 0))
    operands, in_specs = [x2], [row_spec(d)]
    est = 2 * 2 * tm * d * 4 + 2 * d * d_ff * 2 + tm * d_ff * 2 + tm * d * 4
    scratch = [pltpu.VMEM((tm, d_ff), BF16), pltpu.VMEM((tm, d), F32)]
    if attn is not None:
        a2, w_o, mix_post_g = attn
        operands += [a2, w_o, mix_post_g]
        in_specs += [row_spec(a2.shape[1]), _const_spec(w_o.shape), _const_spec((1, d))]
        scratch.append(pltpu.VMEM((tm, d), F32))
        est += 2 * tm * a2.shape[1] * 2 + w_o.size * 2 + tm * d * 4
    operands += [pre_g, post_g, w_up, w_dn]
    in_specs += [_const_spec((1, d)), _const_spec((1, d)), _const_spec((d, d_ff)), _const_spec((d_ff, d))]
    out_specs, out_shape = [row_spec(d)], [jax.ShapeDtypeStruct((t, d), F32)]
    for g, w, n_out, _ in projs:
        width = w.shape[1] // n_out
        operands += [g, w]
        in_specs += [_const_spec((1, d)), _const_spec(w.shape)]
        out_specs += [row_spec(width)] * n_out
        out_shape += [jax.ShapeDtypeStruct((t, width), BF16)] * n_out
        est += w.size * 2 + n_out * 2 * tm * width * 2
    outs = pl.pallas_call(
        functools.partial(_mlp_kernel, fc=fc, nc=nc, has_attn=attn is not None,
                          proj_outs=tuple(p[2] for p in projs), proj_scales=tuple(p[3] for p in projs)),
        grid=(t // tm,),
        in_specs=in_specs,
        out_specs=out_specs,
        out_shape=out_shape,
        scratch_shapes=scratch,
        compiler_params=pltpu.CompilerParams(
            dimension_semantics=("arbitrary",), vmem_limit_bytes=_vmem_limit(est)),
        name="mlp_block",
    )(*operands)
    rest = list(outs[1:])
    return outs[0], [[rest.pop(0) for _ in range(p[2])] for p in projs]


def _pool_kernel(x_ref, halo_ref, pre_g_ref, post_g_ref, pw_ref, scale_ref, o_ref, hext_ref, m_ref):
    i = pl.program_id(1)
    tm, d = x_ref.shape[1], x_ref.shape[2]
    gc = d // len(POOL_WINDOWS)
    pre_g = pre_g_ref[...]
    h = _rms(x_ref[0], pre_g)
    keep = (i > 0).astype(F32)
    hext_ref[0:HALO, :] = _rms(halo_ref[0], pre_g) * keep
    hext_ref[HALO:HALO + tm, :] = h
    pos = i * tm + lax.broadcasted_iota(jnp.int32, (tm, 1), 0)
    ss = jnp.zeros((tm, 1), F32)
    for g, w in enumerate(POOL_WINDOWS):
        lo, hi = g * gc, (g + 1) * gc
        cur = hext_ref[HALO:HALO + tm, lo:hi]
        acc = cur
        for s in range(1, w):
            acc = acc + hext_ref[HALO - s:HALO - s + tm, lo:hi]
        count = jnp.minimum(pos + 1, w).astype(F32)
        y = acc * (1.0 / count) - cur
        m = jnp.dot(y.astype(BF16), pw_ref[g], preferred_element_type=F32) * scale_ref[:, lo:hi]
        m_ref[:, lo:hi] = m
        ss = ss + jnp.sum(m * m, axis=-1, keepdims=True)
    inv = lax.rsqrt(ss * (1.0 / d) + EPS)
    o_ref[0] = x_ref[0] + m_ref[...] * inv * post_g_ref[...]


def _pool_block(x, pre_g, post_g, pw, scale, *, tm=512):
    b, s, d = x.shape
    g, gc = pw.shape[0], pw.shape[1]
    per_tile = tm // HALO
    est = 2 * 2 * tm * d * 4 + 2 * HALO * d * 4 + g * gc * gc * 2 + (2 * tm + HALO) * d * 4
    return pl.pallas_call(
        _pool_kernel,
        grid=(b, s // tm),
        in_specs=[
            pl.BlockSpec((1, tm, d), lambda bi, i: (bi, i, 0)),
            pl.BlockSpec((1, HALO, d), lambda bi, i: (bi, jnp.maximum(i * per_tile - 1, 0), 0)),
            _const_spec((1, d)),
            _const_spec((1, d)),
            _const_spec((g, gc, gc)),
            _const_spec((1, d)),
        ],
        out_specs=pl.BlockSpec((1, tm, d), lambda bi, i: (bi, i, 0)),
        out_shape=jax.ShapeDtypeStruct((b, s, d), F32),
        scratch_shapes=[pltpu.VMEM((HALO + tm, d), F32), pltpu.VMEM((tm, d), F32)],
        compiler_params=pltpu.CompilerParams(
            dimension_semantics=("arbitrary", "arbitrary"), vmem_limit_bytes=_vmem_limit(est)),
        name="pool_block",
    )(x, x, pre_g, post_g, pw, scale)


def _softplus2(y):
    return jnp.maximum(y, 0.0) + jnp.log(1.0 + jnp.exp2(-jnp.abs(y))) * LOG2E


def _attn_kernel(q_ref, k_ref, v_ref, cum_ref, o_ref, acc_ref, carry_ref, sp_ref, z_ref, a_ref,
                 *, tq, dh, window):
    s, width = q_ref.shape[1], q_ref.shape[2]
    n_tiles = s // tq
    n_pairs = width // LANES
    row = lax.broadcasted_iota(jnp.int32, (tq, tq), 0)
    col = lax.broadcasted_iota(jnp.int32, (tq, tq), 1)
    causal = col < row
    lane = lax.broadcasted_iota(jnp.int32, (tq, LANES), 1)
    head_lanes = (lane < dh, lane >= dh)

    def old_block(qh, kb, carry):
        z = lax.dot_general(qh, kb, (((1,), (1,)), ((), ())), preferred_element_type=F32)
        sp = _softplus2(z)
        cum = jnp.dot(sp.astype(BF16), cum_ref[0:tq, 0:tq], preferred_element_type=F32)
        return z + cum + carry, carry - jnp.sum(sp, axis=1, keepdims=True)

    def scores(i, n_prev):
        q0 = pl.multiple_of(i * tq, tq)
        k0 = pl.multiple_of((i - n_prev) * tq, tq)
        nb = n_prev + 1
        live = None
        for p in range(n_pairs):
            cols = slice(p * LANES, (p + 1) * LANES)
            qp = q_ref[0, pl.ds(q0, tq), cols]
            kw = k_ref[0, pl.ds(k0, nb * tq), cols]
            for hh in range(2):
                qh = jnp.where(head_lanes[hh], qp, jnp.zeros_like(qp))
                z = lax.dot_general(qh, kw, (((1,), (1,)), ((), ())), preferred_element_type=F32)
                for b in range(nb):
                    r0 = (b * 2 + hh) * tq
                    z_ref[r0:r0 + tq, cols] = z[:, b * tq:(b + 1) * tq]
        for p in range(n_pairs):
            cols = slice(p * LANES, (p + 1) * LANES)
            for hh in range(2):
                carry = None
                for b in range(n_prev, -1, -1):
                    r0 = (b * 2 + hh) * tq
                    zb = z_ref[r0:r0 + tq, cols]
                    if b == n_prev:
                        zb = jnp.where(causal, zb, NEG_BIG)
                    sp = _softplus2(zb)
                    arg = zb if carry is None else zb + carry
                    sp_ref[r0:r0 + tq, cols] = sp.astype(BF16)
                    z_ref[r0:r0 + tq, cols] = arg
                    total = jnp.sum(sp, axis=1, keepdims=True)
                    carry = -total if carry is None else carry - total
                carry_ref[2 * p + hh] = jnp.broadcast_to(carry, (tq, LANES))
                live = carry if live is None else jnp.maximum(live, carry)
        return jnp.max(live) > DEAD_LOG2

    def weights_values(i, n_prev):
        q0 = pl.multiple_of(i * tq, tq)
        k0 = pl.multiple_of((i - n_prev) * tq, tq)
        nb = n_prev + 1
        cw = cum_ref.shape[0]
        for b in range(nb):
            blk = slice(b * 2 * tq, (b + 1) * 2 * tq)
            for c0 in range(0, width, cw):
                cum = jnp.dot(sp_ref[blk, c0:c0 + cw], cum_ref[...], preferred_element_type=F32)
                a_ref[blk, c0:c0 + cw] = jnp.exp2(z_ref[blk, c0:c0 + cw] + cum).astype(BF16)
        accs = []
        for p in range(n_pairs):
            cols = slice(p * LANES, (p + 1) * LANES)
            acc = None
            for b in range(nb):
                r0 = b * 2 * tq
                vb = v_ref[0, pl.ds(pl.multiple_of(k0 + b * tq, tq), tq), cols]
                part = jnp.dot(a_ref[r0:r0 + 2 * tq, cols], vb, preferred_element_type=F32)
                acc = part if acc is None else acc + part
            accs.append(acc)
            o_ref[0, pl.ds(q0, tq), cols] = jnp.where(head_lanes[0], acc[:tq], acc[tq:]).astype(o_ref.dtype)
        return accs

    def finish(i, first_old, live, accs):
        def run():
            for p in range(n_pairs):
                for hh in range(2):
                    acc_ref[2 * p + hh] = accs[p][hh * tq:(hh + 1) * tq]

            def cond(st):
                return jnp.logical_and(st[0] >= 0, st[1])

            def body(st):
                j = st[0]
                k0 = pl.multiple_of(j * tq, tq)
                q0 = pl.multiple_of(i * tq, tq)
                mx = None
                for p in range(n_pairs):
                    cols = slice(p * LANES, (p + 1) * LANES)
                    qp = q_ref[0, pl.ds(q0, tq), cols]
                    kb = k_ref[0, pl.ds(k0, tq), cols]
                    vb = v_ref[0, pl.ds(k0, tq), cols]
                    for hh in range(2):
                        h = 2 * p + hh
                        qh = jnp.where(head_lanes[hh], qp, jnp.zeros_like(qp))
                        arg, carry = old_block(qh, kb, carry_ref[h])
                        acc_ref[h] += jnp.dot(jnp.exp2(arg).astype(BF16), vb, preferred_element_type=F32)
                        carry_ref[h] = carry
                        mx = carry if mx is None else jnp.maximum(mx, carry)
                return j - 1, jnp.max(mx) > DEAD_LOG2

            lax.while_loop(cond, body, (first_old, jnp.bool_(True)))
            q0 = pl.multiple_of(i * tq, tq)
            for p in range(n_pairs):
                cols = slice(p * LANES, (p + 1) * LANES)
                o_ref[0, pl.ds(q0, tq), cols] = jnp.where(
                    head_lanes[0], acc_ref[2 * p], acc_ref[2 * p + 1]).astype(o_ref.dtype)

        pl.when(jnp.logical_and(first_old >= 0, live))(run)

    for i in range(window):
        scores(i, i)
        weights_values(i, i)

    def tile_body(i, c):
        live = scores(i, window)
        accs = weights_values(i, window)
        finish(i, i - window - 1, live, accs)
        return c

    lax.fori_loop(window, n_tiles, tile_body, 0)


def _attention(q, k, v, *, tq=128, window=2, heads_per_step=8):
    b, s, hd = q.shape
    dh = hd // N_HEADS
    assert 2 * dh == LANES and tq == LANES
    assert s // tq > window
    width = heads_per_step * dh
    cw = min(width, V7X_MXU_DIM)
    jj = lax.broadcasted_iota(jnp.int32, (cw, cw), 0)
    ss = lax.broadcasted_iota(jnp.int32, (cw, cw), 1)
    cum_mat = jnp.where((jj >= ss) & (jj // tq == ss // tq), -1.0, 0.0).astype(BF16)
    staged_rows = 2 * (window + 1) * tq
    est = (4 * 2 * s * width * 2 + cw * cw * 2 + 2 * heads_per_step * tq * LANES * 4
           + staged_rows * width * (2 + 4 + 2))
    seq_spec = pl.BlockSpec((1, s, width), lambda bi, g: (bi, 0, g))
    return pl.pallas_call(
        functools.partial(_attn_kernel, tq=tq, dh=dh, window=window),
        grid=(b, hd // width),
        in_specs=[seq_spec, seq_spec, seq_spec, _const_spec((cw, cw))],
        out_specs=seq_spec,
        out_shape=jax.ShapeDtypeStruct((b, s, hd), BF16),
        scratch_shapes=[pltpu.VMEM((heads_per_step, tq, LANES), F32),
                        pltpu.VMEM((heads_per_step, tq, LANES), F32),
                        pltpu.VMEM((staged_rows, width), BF16),
                        pltpu.VMEM((staged_rows, width), F32),
                        pltpu.VMEM((staged_rows, width), BF16)],
        compiler_params=pltpu.CompilerParams(
            dimension_semantics=("arbitrary", "arbitrary"), vmem_limit_bytes=_vmem_limit(est)),
        name="stickbreak_attn",
    )(q, k, v, cum_mat)


def kernel(x, pool_w, pool_scale, w_q, w_kv, kv_norm_g, w_o, w_up, w_down,
           mix_pre_g, mix_post_g, mlp_pre_g, mlp_post_g):
    b, s, d = x.shape
    depth = w_up.shape[0]
    n_a = pool_w.shape[0]
    hd = w_q.shape[2]
    q_scale = float((hd // N_HEADS) ** -0.5) * LOG2E
    assert 1 <= n_a < depth
    row = lambda g: g.reshape(1, d).astype(F32)
    q = k = v = None
    for layer in range(depth):
        attn = None
        if layer < n_a:
            x = _pool_block(x, row(mix_pre_g[layer]), row(mix_post_g[layer]),
                            pool_w[layer].astype(BF16), row(pool_scale[layer]))
        else:
            o = _attention(q.reshape(b, s, hd), k.reshape(b, s, hd), v.reshape(b, s, hd))
            attn = (o.reshape(b * s, hd), w_o[layer - n_a].astype(BF16), row(mix_post_g[layer]))
        projs = []
        if layer == n_a - 1:
            projs.append((row(kv_norm_g), w_kv.astype(BF16), 2, 1.0))
        if n_a <= layer + 1 < depth:
            projs.append((row(mix_pre_g[layer + 1]), w_q[layer + 1 - n_a].astype(BF16), 1, q_scale))
        x2, outs = _mlp_block(x.reshape(b * s, d), row(mlp_pre_g[layer]), row(mlp_post_g[layer]),
                              w_up[layer].astype(BF16), w_down[layer].astype(BF16), attn=attn, projs=projs)
        x = x2.reshape(b, s, d)
        if layer == n_a - 1:
            k, v = outs.pop(0)
        if outs:
            (q,) = outs.pop(0)
    return x
```

```python
import functools

import jax
import jax.numpy as jnp
from jax import lax
from jax.experimental import pallas as pl
from jax.experimental.pallas import tpu as pltpu

F32 = jnp.float32
BF16 = jnp.bfloat16

EPS = 1e-6
POOL_WINDOWS = (2, 4, 8, 16)
N_HEADS = 16
HALO = 16
LANES = 128
V7X_VMEM_BYTES = 64 * 1024 * 1024
V7X_MXU_DIM = 256
NEG_BIG = -1e30
DEAD_LOG2 = -150.0
DEAD_LOG2_BF16_SUM = DEAD_LOG2 * (1.0 + 2.0 ** -8)
LOG2E = 1.4426950408889634


def _vmem_limit(est_bytes):
    return int(min(V7X_VMEM_BYTES - 4 * 1024 * 1024, est_bytes * 3 // 2 + 8 * 1024 * 1024))


def _rms(x, g):
    ms = jnp.mean(x * x, axis=-1, keepdims=True)
    return x * lax.rsqrt(ms + EPS) * g


def _const_spec(shape):
    nd = len(shape)
    return pl.BlockSpec(shape, lambda *_: (0,) * nd, pipeline_mode=pl.Buffered(1))


def _dot_rms_update(lhs_ref, w_ref, g_ref, res_ref, y_ref, dst_ref, nc):
    tm, d = y_ref.shape
    ss = jnp.zeros((tm, 1), F32)
    for n in range(d // nc):
        y = jnp.dot(lhs_ref[...], w_ref[:, n * nc:(n + 1) * nc], preferred_element_type=F32)
        y_ref[:, n * nc:(n + 1) * nc] = y
        ss = ss + jnp.sum(y * y, axis=-1, keepdims=True)
    inv = lax.rsqrt(ss * (1.0 / d) + EPS)
    dst_ref[...] = res_ref[...] + y_ref[...] * inv * g_ref[...]


def _mlp_kernel(*refs, fc, nc, has_attn, proj_outs, proj_scales):
    refs = list(refs)
    x_ref = refs.pop(0)
    if has_attn:
        attn_ref, wo_ref, mix_post_g_ref = refs.pop(0), refs.pop(0), refs.pop(0)
    pre_g_ref, post_g_ref, wup_ref, wdn_ref = (refs.pop(0) for _ in range(4))
    projs = [(refs.pop(0), refs.pop(0)) for _ in proj_outs]
    o_ref = refs.pop(0)
    proj_refs = [[refs.pop(0) for _ in range(n)] for n in proj_outs]
    a_ref, y_ref = refs.pop(0), refs.pop(0)
    d_ff = wup_ref.shape[-1]

    if has_attn:
        x1_ref = refs.pop(0)
        _dot_rms_update(attn_ref, wo_ref, mix_post_g_ref, x_ref, y_ref, x1_ref, nc)
    else:
        x1_ref = x_ref
    h = _rms(x1_ref[...], pre_g_ref[...]).astype(BF16)
    for c in range(d_ff // fc):
        u = jnp.dot(h, wup_ref[:, c * fc:(c + 1) * fc], preferred_element_type=F32)
        r = jnp.maximum(u, 0.0)
        a_ref[:, c * fc:(c + 1) * fc] = (r * r).astype(BF16)
    _dot_rms_update(a_ref, wdn_ref, post_g_ref, x1_ref, y_ref, o_ref, nc)
    for (g_ref, w_ref), outs, scale in zip(projs, proj_refs, proj_scales):
        hn = _rms(o_ref[...], g_ref[...]).astype(BF16)
        width = outs[0].shape[-1]
        for j, p_ref in enumerate(outs):
            for n in range(width // nc):
                c0 = j * width + n * nc
                y = jnp.dot(hn, w_ref[:, c0:c0 + nc], preferred_element_type=F32)
                p_ref[:, n * nc:(n + 1) * nc] = (y * scale if scale != 1.0 else y).astype(p_ref.dtype)


def _mlp_block(x2, pre_g, post_g, w_up, w_dn, *, attn=None, projs=(), tm=512, fc=512, nc=256):
    t, d = x2.shape
    d_ff = w_up.shape[1]
    row_spec = lambda n: pl.BlockSpec((tm, n), lambda i: (i, 0))
    operands, in_specs = [x2], [row_spec(d)]
    est = 2 * 2 * tm * d * 4 + 2 * d * d_ff * 2 + tm * d_ff * 2 + tm * d * 4
    scratch = [pltpu.VMEM((tm, d_ff), BF16), pltpu.VMEM((tm, d), F32)]
    if attn is not None:
        a2, w_o, mix_post_g = attn
        operands += [a2, w_o, mix_post_g]
        in_specs += [row_spec(a2.shape[1]), _const_spec(w_o.shape), _const_spec((1, d))]
        scratch.append(pltpu.VMEM((tm, d), F32))
        est += 2 * tm * a2.shape[1] * 2 + w_o.size * 2 + tm * d * 4
    operands += [pre_g, post_g, w_up, w_dn]
    in_specs += [_const_spec((1, d)), _const_spec((1, d)), _const_spec((d, d_ff)), _const_spec((d_ff, d))]
    out_specs, out_shape = [row_spec(d)], [jax.ShapeDtypeStruct((t, d), F32)]
    for g, w, n_out, _ in projs:
        width = w.shape[1] // n_out
        operands += [g, w]
        in_specs += [_const_spec((1, d)), _const_spec(w.shape)]
        out_specs += [row_spec(width)] * n_out
        out_shape += [jax.ShapeDtypeStruct((t, width), BF16)] * n_out
        est += w.size * 2 + n_out * 2 * tm * width * 2
    outs = pl.pallas_call(
        functools.partial(_mlp_kernel, fc=fc, nc=nc, has_attn=attn is not None,
                          proj_outs=tuple(p[2] for p in projs), proj_scales=tuple(p[3] for p in projs)),
        grid=(t // tm,),
        in_specs=in_specs,
        out_specs=out_specs,
        out_shape=out_shape,
        scratch_shapes=scratch,
        compiler_params=pltpu.CompilerParams(
            dimension_semantics=("arbitrary",), vmem_limit_bytes=_vmem_limit(est)),
        name="mlp_block",
    )(*operands)
    rest = list(outs[1:])
    return outs[0], [[rest.pop(0) for _ in range(p[2])] for p in projs]


def _pool_kernel(x_ref, halo_ref, pre_g_ref, post_g_ref, pw_ref, scale_ref, o_ref, hext_ref, m_ref):
    i = pl.program_id(1)
    tm, d = x_ref.shape[1], x_ref.shape[2]
    gc = d // len(POOL_WINDOWS)
    pre_g = pre_g_ref[...]
    h = _rms(x_ref[0], pre_g)
    keep = (i > 0).astype(F32)
    hext_ref[0:HALO, :] = _rms(halo_ref[0], pre_g) * keep
    hext_ref[HALO:HALO + tm, :] = h
    pos = i * tm + lax.broadcasted_iota(jnp.int32, (tm, 1), 0)
    ss = jnp.zeros((tm, 1), F32)
    for g, w in enumerate(POOL_WINDOWS):
        lo, hi = g * gc, (g + 1) * gc
        cur = hext_ref[HALO:HALO + tm, lo:hi]
        acc = cur
        for s in range(1, w):
            acc = acc + hext_ref[HALO - s:HALO - s + tm, lo:hi]
        count = jnp.minimum(pos + 1, w).astype(F32)
        y = acc * (1.0 / count) - cur
        m = jnp.dot(y.astype(BF16), pw_ref[g], preferred_element_type=F32) * scale_ref[:, lo:hi]
        m_ref[:, lo:hi] = m
        ss = ss + jnp.sum(m * m, axis=-1, keepdims=True)
    inv = lax.rsqrt(ss * (1.0 / d) + EPS)
    o_ref[0] = x_ref[0] + m_ref[...] * inv * post_g_ref[...]


def _pool_block(x, pre_g, post_g, pw, scale, *, tm=512):
    b, s, d = x.shape
    g, gc = pw.shape[0], pw.shape[1]
    per_tile = tm // HALO
    est = 2 * 2 * tm * d * 4 + 2 * HALO * d * 4 + g * gc * gc * 2 + (2 * tm + HALO) * d * 4
    return pl.pallas_call(
        _pool_kernel,
        grid=(b, s // tm),
        in_specs=[
            pl.BlockSpec((1, tm, d), lambda bi, i: (bi, i, 0)),
            pl.BlockSpec((1, HALO, d), lambda bi, i: (bi, jnp.maximum(i * per_tile - 1, 0), 0)),
            _const_spec((1, d)),
            _const_spec((1, d)),
            _const_spec((g, gc, gc)),
            _const_spec((1, d)),
        ],
        out_specs=pl.BlockSpec((1, tm, d), lambda bi, i: (bi, i, 0)),
        out_shape=jax.ShapeDtypeStruct((b, s, d), F32),
        scratch_shapes=[pltpu.VMEM((HALO + tm, d), F32), pltpu.VMEM((tm, d), F32)],
        compiler_params=pltpu.CompilerParams(
            dimension_semantics=("arbitrary", "arbitrary"), vmem_limit_bytes=_vmem_limit(est)),
        name="pool_block",
    )(x, x, pre_g, post_g, pw, scale)


def _softplus2(y):
    return jnp.maximum(y, 0.0) + jnp.log(1.0 + jnp.exp2(-jnp.abs(y))) * LOG2E


def _attn_kernel(q_ref, k_ref, v_ref, tri_ref, o_ref, acc_ref, carry_ref, sp_ref, z_ref, a_ref,
                 *, tq, dh, tiles_per_trip):
    s, width = q_ref.shape[1], q_ref.shape[2]
    win = tri_ref.shape[0]
    n_tiles = s // tq
    lead = win // tq - 1
    n_pairs = width // LANES
    lane = lax.broadcasted_iota(jnp.int32, (tq, LANES), 1)
    head_lanes = (lane < dh, lane >= dh)

    def visible(offset, c0, c1):
        r = lax.broadcasted_iota(jnp.int32, (tq, c1 - c0), 0)
        c = lax.broadcasted_iota(jnp.int32, (tq, c1 - c0), 1) + c0
        return c < r + offset

    def window_tiles(tiles):
        for n, (q0, k0, mask) in enumerate(tiles):
            for p in range(n_pairs):
                cols = slice(p * LANES, (p + 1) * LANES)
                qp = q_ref[0, pl.ds(q0, tq), cols]
                kw = k_ref[0, pl.ds(k0, win), cols]
                for hh in range(2):
                    r0 = ((n * n_pairs + p) * 2 + hh) * tq
                    qh = jnp.where(head_lanes[hh], qp, jnp.zeros_like(qp))
                    y = mask(lax.dot_general(qh, kw, (((1,), (1,)), ((), ())), preferred_element_type=F32))
                    z_ref[r0:r0 + tq, :] = y
                    sp_ref[r0:r0 + tq, :] = _softplus2(y).astype(BF16)
        rows = len(tiles) * 2 * n_pairs * tq
        cum = jnp.dot(sp_ref[0:rows, :], tri_ref[...], preferred_element_type=F32)
        a_ref[0:rows, :] = jnp.exp2(z_ref[0:rows, :] + cum).astype(BF16)
        live = []
        for n, (q0, k0, _) in enumerate(tiles):
            t0 = n * 2 * n_pairs * tq
            live.append(jnp.max(cum[t0:t0 + 2 * n_pairs * tq, 0:1]) > DEAD_LOG2_BF16_SUM)
            for p in range(n_pairs):
                cols = slice(p * LANES, (p + 1) * LANES)
                r0 = t0 + 2 * p * tq
                acc = jnp.dot(a_ref[r0:r0 + 2 * tq, :], v_ref[0, pl.ds(k0, win), cols],
                              preferred_element_type=F32)
                o_ref[0, pl.ds(q0, tq), cols] = jnp.where(head_lanes[0], acc[:tq], acc[tq:]).astype(o_ref.dtype)
        return live

    def whole_prefix_tile(q0):
        acc_ref[...] = jnp.zeros_like(acc_ref)
        carry_ref[...] = jnp.zeros_like(carry_ref)
        row = q0 + lax.broadcasted_iota(jnp.int32, (tq, LANES), 0)
        col = lax.broadcasted_iota(jnp.int32, (tq, LANES), 1)

        def cond(st):
            return jnp.logical_and(st[0] >= 0, st[1])

        def body(st):
            j = st[0]
            k0 = pl.multiple_of(j * LANES, LANES)
            seen = k0 + col < row
            mx = None
            for p in range(n_pairs):
                cols = slice(p * LANES, (p + 1) * LANES)
                qp = q_ref[0, pl.ds(q0, tq), cols]
                kb = k_ref[0, pl.ds(k0, LANES), cols]
                vb = v_ref[0, pl.ds(k0, LANES), cols]
                for hh in range(2):
                    h = 2 * p + hh
                    qh = jnp.where(head_lanes[hh], qp, jnp.zeros_like(qp))
                    z = lax.dot_general(qh, kb, (((1,), (1,)), ((), ())), preferred_element_type=F32)
                    y = jnp.where(seen, z, NEG_BIG)
                    sp = _softplus2(y)
                    cum = jnp.dot(sp.astype(BF16), tri_ref[0:LANES, 0:LANES], preferred_element_type=F32)
                    carry = carry_ref[h]
                    acc_ref[h] += jnp.dot(jnp.exp2(y + cum + carry).astype(BF16), vb, preferred_element_type=F32)
                    carry = carry - jnp.sum(sp, axis=1, keepdims=True)
                    carry_ref[h] = carry
                    mx = carry if mx is None else jnp.maximum(mx, carry)
            return j - 1, jnp.max(mx) > DEAD_LOG2

        lax.while_loop(cond, body, ((q0 + tq - 1) // LANES, jnp.bool_(True)))
        for p in range(n_pairs):
            cols = slice(p * LANES, (p + 1) * LANES)
            o_ref[0, pl.ds(q0, tq), cols] = jnp.where(
                head_lanes[0], acc_ref[2 * p], acc_ref[2 * p + 1]).astype(o_ref.dtype)

    diag = visible(lead * tq, win - LANES, win)

    def steady_mask(y):
        return jnp.concatenate([y[:, :win - LANES], jnp.where(diag, y[:, win - LANES:], NEG_BIG)], axis=1)

    def early_mask(i):
        return lambda y: jnp.where(visible(i * tq, 0, win), y, NEG_BIG)

    first = n_tiles - (n_tiles - lead - 1) // tiles_per_trip * tiles_per_trip
    live = window_tiles([(i * tq, max(i - lead, 0) * tq, early_mask(i) if i < lead else steady_mask)
                         for i in range(first)])
    for i in range(lead + 1, first):
        pl.when(live[i])(functools.partial(whole_prefix_tile, i * tq))

    def trip_body(n, c):
        q0s = [pl.multiple_of((first + n * tiles_per_trip + m) * tq, tq) for m in range(tiles_per_trip)]
        live = window_tiles([(q0, pl.multiple_of(q0 - lead * tq, tq), steady_mask) for q0 in q0s])
        for m in range(tiles_per_trip):
            pl.when(live[m])(functools.partial(whole_prefix_tile, q0s[m]))
        return c

    lax.fori_loop(0, (n_tiles - first) // tiles_per_trip, trip_body, 0)


def _attention(q, k, v, *, tq=64, heads_per_step=8, tiles_per_trip=8):
    b, s, hd = q.shape
    dh = hd // N_HEADS
    win = V7X_MXU_DIM
    assert 2 * dh == LANES and win % tq == 0 and LANES % tq == 0 and s % tq == 0 and s >= win
    width = heads_per_step * dh
    jj = lax.broadcasted_iota(jnp.int32, (win, win), 0)
    ss = lax.broadcasted_iota(jnp.int32, (win, win), 1)
    tri = jnp.where(jj >= ss, -1.0, 0.0).astype(BF16)
    staged_rows = (tiles_per_trip + win // tq) * heads_per_step * tq
    est = (4 * 2 * s * width * 2 + win * win * 2 + 2 * heads_per_step * tq * LANES * 4
           + staged_rows * win * (2 + 4 + 2))
    seq_spec = pl.BlockSpec((1, s, width), lambda bi, g: (bi, 0, g))
    return pl.pallas_call(
        functools.partial(_attn_kernel, tq=tq, dh=dh, tiles_per_trip=tiles_per_trip),
        grid=(b, hd // width),
        in_specs=[seq_spec, seq_spec, seq_spec, _const_spec((win, win))],
        out_specs=seq_spec,
        out_shape=jax.ShapeDtypeStruct((b, s, hd), BF16),
        scratch_shapes=[pltpu.VMEM((heads_per_step, tq, LANES), F32),
                        pltpu.VMEM((heads_per_step, tq, LANES), F32),
                        pltpu.VMEM((staged_rows, win), BF16),
                        pltpu.VMEM((staged_rows, win), F32),
                        pltpu.VMEM((staged_rows, win), BF16)],
        compiler_params=pltpu.CompilerParams(
            dimension_semantics=("arbitrary", "arbitrary"), vmem_limit_bytes=_vmem_limit(est)),
        name="stickbreak_attn",
    )(q, k, v, tri)


def kernel(x, pool_w, pool_scale, w_q, w_kv, kv_norm_g, w_o, w_up, w_down,
           mix_pre_g, mix_post_g, mlp_pre_g, mlp_post_g):
    b, s, d = x.shape
    depth = w_up.shape[0]
    n_a = pool_w.shape[0]
    hd = w_q.shape[2]
    q_scale = float((hd // N_HEADS) ** -0.5) * LOG2E
    assert 1 <= n_a < depth
    row = lambda g: g.reshape(1, d).astype(F32)
    q = k = v = None
    for layer in range(depth):
        attn = None
        if layer < n_a:
            x = _pool_block(x, row(mix_pre_g[layer]), row(mix_post_g[layer]),
                            pool_w[layer].astype(BF16), row(pool_scale[layer]))
        else:
            o = _attention(q.reshape(b, s, hd), k.reshape(b, s, hd), v.reshape(b, s, hd))
            attn = (o.reshape(b * s, hd), w_o[layer - n_a].astype(BF16), row(mix_post_g[layer]))
        projs = []
        if layer == n_a - 1:
            projs.append((row(kv_norm_g), w_kv.astype(BF16), 2, 1.0))
        if n_a <= layer + 1 < depth:
            projs.append((row(mix_pre_g[layer + 1]), w_q[layer + 1 - n_a].astype(BF16), 1, q_scale))
        x2, outs = _mlp_block(x.reshape(b * s, d), row(mlp_pre_g[layer]), row(mlp_post_g[layer]),
                              w_up[layer].astype(BF16), w_down[layer].astype(BF16), attn=attn, projs=projs)
        x = x2.reshape(b, s, d)
        if layer == n_a - 1:
            k, v = outs.pop(0)
        if outs:
            (q,) = outs.pop(0)
    return x
```

```python
import functools

import jax
import jax.numpy as jnp
from jax import lax
from jax.experimental import pallas as pl
from jax.experimental.pallas import tpu as pltpu

F32 = jnp.float32
BF16 = jnp.bfloat16

EPS = 1e-6
POOL_WINDOWS = (2, 4, 8, 16)
N_HEADS = 16
HALO = 16
PAD = 8
LANES = 128
V7X_VMEM_BYTES = 64 * 1024 * 1024
V7X_MXU_DIM = 256
NEG_BIG = -1e30
DEAD_LOG2 = -150.0
LOG2E = 1.4426950408889634


def _vmem_limit(est_bytes):
    return int(min(V7X_VMEM_BYTES - 4 * 1024 * 1024, est_bytes * 3 // 2 + 8 * 1024 * 1024))


def _rms(x, g):
    ms = jnp.mean(x * x, axis=-1, keepdims=True)
    return x * lax.rsqrt(ms + EPS) * g


def _const_spec(shape):
    nd = len(shape)
    return pl.BlockSpec(shape, lambda *_: (0,) * nd, pipeline_mode=pl.Buffered(1))


def _dot_rms_update(lhs_ref, w_ref, g_ref, res_ref, y_ref, dst_ref, nc):
    tm, d = y_ref.shape
    ss = jnp.zeros((tm, 1), F32)
    for n in range(d // nc):
        y = jnp.dot(lhs_ref[...], w_ref[:, n * nc:(n + 1) * nc], preferred_element_type=F32)
        y_ref[:, n * nc:(n + 1) * nc] = y
        ss = ss + jnp.sum(y * y, axis=-1, keepdims=True)
    inv = lax.rsqrt(ss * (1.0 / d) + EPS)
    dst_ref[...] = res_ref[...] + y_ref[...] * inv * g_ref[...]


def _mlp_kernel(*refs, fc, nc, has_attn, proj_outs, proj_scales):
    refs = list(refs)
    x_ref = refs.pop(0)
    if has_attn:
        attn_ref, wo_ref, mix_post_g_ref = refs.pop(0), refs.pop(0), refs.pop(0)
    pre_g_ref, post_g_ref, wup_ref, wdn_ref = (refs.pop(0) for _ in range(4))
    projs = [(refs.pop(0), refs.pop(0)) for _ in proj_outs]
    o_ref = refs.pop(0)
    proj_refs = [[refs.pop(0) for _ in range(n)] for n in proj_outs]
    a_ref, y_ref = refs.pop(0), refs.pop(0)
    d_ff = wup_ref.shape[-1]

    if has_attn:
        x1_ref = refs.pop(0)
        _dot_rms_update(attn_ref, wo_ref, mix_post_g_ref, x_ref, y_ref, x1_ref, nc)
    else:
        x1_ref = x_ref
    h = _rms(x1_ref[...], pre_g_ref[...]).astype(BF16)
    for c in range(d_ff // fc):
        u = jnp.dot(h, wup_ref[:, c * fc:(c + 1) * fc], preferred_element_type=F32)
        r = jnp.maximum(u, 0.0)
        a_ref[:, c * fc:(c + 1) * fc] = (r * r).astype(BF16)
    _dot_rms_update(a_ref, wdn_ref, post_g_ref, x1_ref, y_ref, o_ref, nc)
    for (g_ref, w_ref), outs, scale in zip(projs, proj_refs, proj_scales):
        hn = _rms(o_ref[...], g_ref[...]).astype(BF16)
        width = outs[0].shape[-1]
        for j, p_ref in enumerate(outs):
            for n in range(width // nc):
                c0 = j * width + n * nc
                y = jnp.dot(hn, w_ref[:, c0:c0 + nc], preferred_element_type=F32)
                p_ref[:, n * nc:(n + 1) * nc] = (y * scale if scale != 1.0 else y).astype(p_ref.dtype)


def _mlp_block(x2, pre_g, post_g, w_up, w_dn, *, attn=None, projs=(), tm=512, fc=512, nc=256):
    t, d = x2.shape
    d_ff = w_up.shape[1]
    row_spec = lambda n: pl.BlockSpec((tm, n), lambda i: (i, 0))
    operands, in_specs = [x2], [row_spec(d)]
    est = 2 * 2 * tm * d * 4 + 2 * d * d_ff * 2 + tm * d_ff * 2 + tm * d * 4
    scratch = [pltpu.VMEM((tm, d_ff), BF16), pltpu.VMEM((tm, d), F32)]
    if attn is not None:
        a2, w_o, mix_post_g = attn
        operands += [a2, w_o, mix_post_g]
        in_specs += [row_spec(a2.shape[1]), _const_spec(w_o.shape), _const_spec((1, d))]
        scratch.append(pltpu.VMEM((tm, d), F32))
        est += 2 * tm * a2.shape[1] * 2 + w_o.size * 2 + tm * d * 4
    operands += [pre_g, post_g, w_up, w_dn]
    in_specs += [_const_spec((1, d)), _const_spec((1, d)), _const_spec((d, d_ff)), _const_spec((d_ff, d))]
    out_specs, out_shape = [row_spec(d)], [jax.ShapeDtypeStruct((t, d), F32)]
    for g, w, n_out, _ in projs:
        width = w.shape[1] // n_out
        operands += [g, w]
        in_specs += [_const_spec((1, d)), _const_spec(w.shape)]
        out_specs += [row_spec(width)] * n_out
        out_shape += [jax.ShapeDtypeStruct((t, width), BF16)] * n_out
        est += w.size * 2 + n_out * 2 * tm * width * 2
    outs = pl.pallas_call(
        functools.partial(_mlp_kernel, fc=fc, nc=nc, has_attn=attn is not None,
                          proj_outs=tuple(p[2] for p in projs), proj_scales=tuple(p[3] for p in projs)),
        grid=(t // tm,),
        in_specs=in_specs,
        out_specs=out_specs,
        out_shape=out_shape,
        scratch_shapes=scratch,
        compiler_params=pltpu.CompilerParams(
            dimension_semantics=("arbitrary",), vmem_limit_bytes=_vmem_limit(est)),
        name="mlp_block",
    )(*operands)
    rest = list(outs[1:])
    return outs[0], [[rest.pop(0) for _ in range(p[2])] for p in projs]


def _pool_kernel(x_ref, halo_ref, pre_g_ref, post_g_ref, pw_ref, scale_ref, o_ref, hext_ref, lva_ref, lvb_ref,
                 m_ref):
    i = pl.program_id(1)
    tm, d = x_ref.shape[1], x_ref.shape[2]
    gc = d // len(POOL_WINDOWS)
    top = PAD + HALO
    pre_g = pre_g_ref[...]
    h = _rms(x_ref[0], pre_g)
    keep = (i > 0).astype(F32)
    for ref in (hext_ref, lva_ref, lvb_ref):
        ref[0:PAD, :] = jnp.zeros((PAD, ref.shape[1]), F32)
    hext_ref[PAD:top, :] = _rms(halo_ref[0], pre_g) * keep
    hext_ref[top:top + tm, :] = h
    pos = i * tm + lax.broadcasted_iota(jnp.int32, (tm, 1), 0)
    ss = jnp.zeros((tm, 1), F32)
    for g, w in enumerate(POOL_WINDOWS):
        lo, hi = g * gc, (g + 1) * gc
        cur = hext_ref[top:top + tm, lo:hi]
        src, src_cols, dst, span = hext_ref, slice(lo, hi), lva_ref, 1
        while 2 * span < w:
            n = HALO + tm
            dst[PAD:PAD + n, :] = src[PAD:PAD + n, src_cols] + src[PAD - span:PAD - span + n, src_cols]
            src, src_cols, dst = dst, slice(0, gc), (lvb_ref if dst is lva_ref else lva_ref)
            span *= 2
        acc = src[top:top + tm, src_cols] + src[top - span:top - span + tm, src_cols]
        count = jnp.minimum(pos + 1, w).astype(F32)
        y = acc * (1.0 / count) - cur
        m = jnp.dot(y.astype(BF16), pw_ref[g], preferred_element_type=F32) * scale_ref[:, lo:hi]
        m_ref[:, lo:hi] = m
        ss = ss + jnp.sum(m * m, axis=-1, keepdims=True)
    inv = lax.rsqrt(ss * (1.0 / d) + EPS)
    o_ref[0] = x_ref[0] + m_ref[...] * inv * post_g_ref[...]


def _pool_block(x, pre_g, post_g, pw, scale, *, tm=512):
    b, s, d = x.shape
    g, gc = pw.shape[0], pw.shape[1]
    per_tile = tm // HALO
    ext = PAD + HALO + tm
    est = 2 * 2 * tm * d * 4 + 2 * HALO * d * 4 + g * gc * gc * 2 + (ext + tm) * d * 4 + 2 * ext * gc * 4
    return pl.pallas_call(
        _pool_kernel,
        grid=(b, s // tm),
        in_specs=[
            pl.BlockSpec((1, tm, d), lambda bi, i: (bi, i, 0)),
            pl.BlockSpec((1, HALO, d), lambda bi, i: (bi, jnp.maximum(i * per_tile - 1, 0), 0)),
            _const_spec((1, d)),
            _const_spec((1, d)),
            _const_spec((g, gc, gc)),
            _const_spec((1, d)),
        ],
        out_specs=pl.BlockSpec((1, tm, d), lambda bi, i: (bi, i, 0)),
        out_shape=jax.ShapeDtypeStruct((b, s, d), F32),
        scratch_shapes=[pltpu.VMEM((ext, d), F32), pltpu.VMEM((ext, gc), F32), pltpu.VMEM((ext, gc), F32),
                        pltpu.VMEM((tm, d), F32)],
        compiler_params=pltpu.CompilerParams(
            dimension_semantics=("arbitrary", "arbitrary"), vmem_limit_bytes=_vmem_limit(est)),
        name="pool_block",
    )(x, x, pre_g, post_g, pw, scale)


def _softplus2(y):
    return jnp.maximum(y, 0.0) + jnp.log(1.0 + jnp.exp2(-jnp.abs(y))) * LOG2E


def _attn_kernel(q_ref, k_ref, v_ref, cum_ref, o_ref, acc_ref, carry_ref, sp_ref, z_ref, a_ref,
                 *, tq, dh, window, tiles_per_trip):
    s, width = q_ref.shape[1], q_ref.shape[2]
    n_tiles = s // tq
    n_pairs = width // LANES
    row = lax.broadcasted_iota(jnp.int32, (tq, tq), 0)
    col = lax.broadcasted_iota(jnp.int32, (tq, tq), 1)
    causal = col < row
    lane = lax.broadcasted_iota(jnp.int32, (tq, LANES), 1)
    head_lanes = (lane < dh, lane >= dh)

    def old_block(qh, kb, carry):
        z = lax.dot_general(qh, kb, (((1,), (1,)), ((), ())), preferred_element_type=F32)
        sp = _softplus2(z)
        cum = jnp.dot(sp.astype(BF16), cum_ref[0:tq, 0:tq], preferred_element_type=F32)
        return z + cum + carry, carry - jnp.sum(sp, axis=1, keepdims=True)

    def scores(i, n_prev, slot):
        q0 = pl.multiple_of(i * tq, tq)
        k0 = pl.multiple_of((i - n_prev) * tq, tq)
        nb = n_prev + 1
        live = None
        for p in range(n_pairs):
            cols = slice(p * LANES, (p + 1) * LANES)
            qp = q_ref[0, pl.ds(q0, tq), cols]
            kw = k_ref[0, pl.ds(k0, nb * tq), cols]
            for hh in range(2):
                qh = jnp.where(head_lanes[hh], qp, jnp.zeros_like(qp))
                z = lax.dot_general(qh, kw, (((1,), (1,)), ((), ())), preferred_element_type=F32)
                for b in range(nb):
                    r0 = (b * 2 + hh) * tq
                    z_ref[slot, r0:r0 + tq, cols] = z[:, b * tq:(b + 1) * tq]
        for p in range(n_pairs):
            cols = slice(p * LANES, (p + 1) * LANES)
            for hh in range(2):
                carry = None
                for b in range(n_prev, -1, -1):
                    r0 = (b * 2 + hh) * tq
                    zb = z_ref[slot, r0:r0 + tq, cols]
                    if b == n_prev:
                        zb = jnp.where(causal, zb, NEG_BIG)
                    sp = _softplus2(zb)
                    arg = zb if carry is None else zb + carry
                    sp_ref[slot, r0:r0 + tq, cols] = sp.astype(BF16)
                    z_ref[slot, r0:r0 + tq, cols] = arg
                    total = jnp.sum(sp, axis=1, keepdims=True)
                    carry = -total if carry is None else carry - total
                carry_ref[slot, 2 * p + hh] = jnp.broadcast_to(carry, (tq, LANES))
                live = carry if live is None else jnp.maximum(live, carry)
        return jnp.max(live) > DEAD_LOG2

    def weights(n_prev, slot):
        cw = cum_ref.shape[0]
        for b in range(n_prev + 1):
            blk = slice(b * 2 * tq, (b + 1) * 2 * tq)
            for c0 in range(0, width, cw):
                cum = jnp.dot(sp_ref[slot, blk, c0:c0 + cw], cum_ref[...], preferred_element_type=F32)
                a_ref[slot, blk, c0:c0 + cw] = jnp.exp2(z_ref[slot, blk, c0:c0 + cw] + cum).astype(BF16)

    def values(i, n_prev, slot):
        q0 = pl.multiple_of(i * tq, tq)
        k0 = pl.multiple_of((i - n_prev) * tq, tq)
        nb = n_prev + 1
        accs = []
        for p in range(n_pairs):
            cols = slice(p * LANES, (p + 1) * LANES)
            acc = None
            for b in range(nb):
                r0 = b * 2 * tq
                vb = v_ref[0, pl.ds(pl.multiple_of(k0 + b * tq, tq), tq), cols]
                part = jnp.dot(a_ref[slot, r0:r0 + 2 * tq, cols], vb, preferred_element_type=F32)
                acc = part if acc is None else acc + part
            accs.append(acc)
            o_ref[0, pl.ds(q0, tq), cols] = jnp.where(head_lanes[0], acc[:tq], acc[tq:]).astype(o_ref.dtype)
        return accs

    def finish(i, first_old, slot, live, accs):
        def run():
            for p in range(n_pairs):
                for hh in range(2):
                    acc_ref[2 * p + hh] = accs[p][hh * tq:(hh + 1) * tq]

            def cond(st):
                return jnp.logical_and(st[0] >= 0, st[1])

            def body(st):
                j = st[0]
                k0 = pl.multiple_of(j * tq, tq)
                q0 = pl.multiple_of(i * tq, tq)
                mx = None
                for p in range(n_pairs):
                    cols = slice(p * LANES, (p + 1) * LANES)
                    qp = q_ref[0, pl.ds(q0, tq), cols]
                    kb = k_ref[0, pl.ds(k0, tq), cols]
                    vb = v_ref[0, pl.ds(k0, tq), cols]
                    for hh in range(2):
                        h = 2 * p + hh
                        qh = jnp.where(head_lanes[hh], qp, jnp.zeros_like(qp))
                        arg, carry = old_block(qh, kb, carry_ref[slot, h])
                        acc_ref[h] += jnp.dot(jnp.exp2(arg).astype(BF16), vb, preferred_element_type=F32)
                        carry_ref[slot, h] = carry
                        mx = carry if mx is None else jnp.maximum(mx, carry)
                return j - 1, jnp.max(mx) > DEAD_LOG2

            lax.while_loop(cond, body, (first_old, jnp.bool_(True)))
            q0 = pl.multiple_of(i * tq, tq)
            for p in range(n_pairs):
                cols = slice(p * LANES, (p + 1) * LANES)
                o_ref[0, pl.ds(q0, tq), cols] = jnp.where(
                    head_lanes[0], acc_ref[2 * p], acc_ref[2 * p + 1]).astype(o_ref.dtype)

        pl.when(jnp.logical_and(first_old >= 0, live))(run)

    for i in range(window):
        scores(i, i, 0)
        weights(i, 0)
        values(i, i, 0)

    def trip_body(n, c):
        tiles = [window + n * tiles_per_trip + m for m in range(tiles_per_trip)]
        live = [scores(i, window, m) for m, i in enumerate(tiles)]
        for m in range(tiles_per_trip):
            weights(window, m)
        accs = [values(i, window, m) for m, i in enumerate(tiles)]
        for m, i in enumerate(tiles):
            finish(i, i - window - 1, m, live[m], accs[m])
        return c

    lax.fori_loop(0, (n_tiles - window) // tiles_per_trip, trip_body, 0)


def _attention(q, k, v, *, tq=128, window=2, heads_per_step=8, tiles_per_trip=3):
    b, s, hd = q.shape
    dh = hd // N_HEADS
    assert 2 * dh == LANES and tq == LANES
    assert s // tq > window and (s // tq - window) % tiles_per_trip == 0
    width = heads_per_step * dh
    cw = min(width, V7X_MXU_DIM)
    jj = lax.broadcasted_iota(jnp.int32, (cw, cw), 0)
    ss = lax.broadcasted_iota(jnp.int32, (cw, cw), 1)
    cum_mat = jnp.where((jj >= ss) & (jj // tq == ss // tq), -1.0, 0.0).astype(BF16)
    staged_rows = 2 * (window + 1) * tq
    est = (4 * 2 * s * width * 2 + cw * cw * 2 + (1 + tiles_per_trip) * heads_per_step * tq * LANES * 4
           + tiles_per_trip * staged_rows * width * (2 + 4 + 2))
    seq_spec = pl.BlockSpec((1, s, width), lambda bi, g: (bi, 0, g))
    return pl.pallas_call(
        functools.partial(_attn_kernel, tq=tq, dh=dh, window=window, tiles_per_trip=tiles_per_trip),
        grid=(b, hd // width),
        in_specs=[seq_spec, seq_spec, seq_spec, _const_spec((cw, cw))],
        out_specs=seq_spec,
        out_shape=jax.ShapeDtypeStruct((b, s, hd), BF16),
        scratch_shapes=[pltpu.VMEM((heads_per_step, tq, LANES), F32),
                        pltpu.VMEM((tiles_per_trip, heads_per_step, tq, LANES), F32),
                        pltpu.VMEM((tiles_per_trip, staged_rows, width), BF16),
                        pltpu.VMEM((tiles_per_trip, staged_rows, width), F32),
                        pltpu.VMEM((tiles_per_trip, staged_rows, width), BF16)],
        compiler_params=pltpu.CompilerParams(
            dimension_semantics=("arbitrary", "arbitrary"), vmem_limit_bytes=_vmem_limit(est)),
        name="stickbreak_attn",
    )(q, k, v, cum_mat)


def kernel(x, pool_w, pool_scale, w_q, w_kv, kv_norm_g, w_o, w_up, w_down,
           mix_pre_g, mix_post_g, mlp_pre_g, mlp_post_g):
    b, s, d = x.shape
    depth = w_up.shape[0]
    n_a = pool_w.shape[0]
    hd = w_q.shape[2]
    q_scale = float((hd // N_HEADS) ** -0.5) * LOG2E
    assert 1 <= n_a < depth
    row = lambda g: g.reshape(1, d).astype(F32)
    q = k = v = None
    for layer in range(depth):
        attn = None
        if layer < n_a:
            x = _pool_block(x, row(mix_pre_g[layer]), row(mix_post_g[layer]),
                            pool_w[layer].astype(BF16), row(pool_scale[layer]))
        else:
            o = _attention(q.reshape(b, s, hd), k.reshape(b, s, hd), v.reshape(b, s, hd))
            attn = (o.reshape(b * s, hd), w_o[layer - n_a].astype(BF16), row(mix_post_g[layer]))
        projs = []
        if layer == n_a - 1:
            projs.append((row(kv_norm_g), w_kv.astype(BF16), 2, 1.0))
        if n_a <= layer + 1 < depth:
            projs.append((row(mix_pre_g[layer + 1]), w_q[layer + 1 - n_a].astype(BF16), 1, q_scale))
        x2, outs = _mlp_block(x.reshape(b * s, d), row(mlp_pre_g[layer]), row(mlp_post_g[layer]),
                              w_up[layer].astype(BF16), w_down[layer].astype(BF16), attn=attn, projs=projs)
        x = x2.reshape(b, s, d)
        if layer == n_a - 1:
            k, v = outs.pop(0)
        if outs:
            (q,) = outs.pop(0)
    return x
```

```python
import functools

import jax
import jax.numpy as jnp
from jax import lax
from jax.experimental import pallas as pl
from jax.experimental.pallas import tpu as pltpu

F32 = jnp.float32
BF16 = jnp.bfloat16

EPS = 1e-6
POOL_WINDOWS = (2, 4, 8, 16)
N_HEADS = 16
HALO = 16
PAD = 8
LANES = 128
V7X_VMEM_BYTES = 64 * 1024 * 1024
V7X_MXU_DIM = 256
NEG_BIG = -1e30
DEAD_LOG2 = -150.0
LOG2E = 1.4426950408889634


def _vmem_limit(est_bytes):
    return int(min(V7X_VMEM_BYTES - 4 * 1024 * 1024, est_bytes * 3 // 2 + 8 * 1024 * 1024))


def _rms(x, g):
    ms = jnp.mean(x * x, axis=-1, keepdims=True)
    return x * lax.rsqrt(ms + EPS) * g


def _const_spec(shape):
    nd = len(shape)
    return pl.BlockSpec(shape, lambda *_: (0,) * nd, pipeline_mode=pl.Buffered(1))


def _layer_spec(stacked_shape, layer):
    nd = len(stacked_shape) - 1
    return pl.BlockSpec((None,) + tuple(stacked_shape[1:]), lambda *_: (layer,) + (0,) * nd,
                        pipeline_mode=pl.Buffered(1))


def _dot_rms_update(lhs_ref, w_ref, g_ref, res_ref, y_ref, dst_ref, nc):
    tm, d = y_ref.shape
    ss = jnp.zeros((tm, 1), F32)
    for n in range(d // nc):
        y = jnp.dot(lhs_ref[...], w_ref[:, n * nc:(n + 1) * nc], preferred_element_type=F32)
        y_ref[:, n * nc:(n + 1) * nc] = y
        ss = ss + jnp.sum(y * y, axis=-1, keepdims=True)
    inv = lax.rsqrt(ss * (1.0 / d) + EPS)
    dst_ref[...] = res_ref[...] + y_ref[...] * inv * g_ref[...]


def _mlp_kernel(*refs, fc, nc, has_attn, proj_outs, proj_scales):
    refs = list(refs)
    x_ref = refs.pop(0)
    if has_attn:
        attn_ref, wo_ref, mix_post_g_ref = refs.pop(0), refs.pop(0), refs.pop(0)
    pre_g_ref, post_g_ref, wup_ref, wdn_ref = (refs.pop(0) for _ in range(4))
    projs = [(refs.pop(0), refs.pop(0)) for _ in proj_outs]
    o_ref = refs.pop(0)
    proj_refs = [[refs.pop(0) for _ in range(n)] for n in proj_outs]
    a_ref, y_ref = refs.pop(0), refs.pop(0)
    d_ff = wup_ref.shape[-1]

    if has_attn:
        x1_ref = refs.pop(0)
        _dot_rms_update(attn_ref, wo_ref, mix_post_g_ref, x_ref, y_ref, x1_ref, nc)
    else:
        x1_ref = x_ref
    h = _rms(x1_ref[...], pre_g_ref[...]).astype(BF16)
    for c in range(d_ff // fc):
        u = jnp.dot(h, wup_ref[:, c * fc:(c + 1) * fc], preferred_element_type=F32)
        r = jnp.maximum(u, 0.0)
        a_ref[:, c * fc:(c + 1) * fc] = (r * r).astype(BF16)
    _dot_rms_update(a_ref, wdn_ref, post_g_ref, x1_ref, y_ref, o_ref, nc)
    for (g_ref, w_ref), outs, scale in zip(projs, proj_refs, proj_scales):
        hn = _rms(o_ref[...], g_ref[...]).astype(BF16)
        width = outs[0].shape[-1]
        for j, p_ref in enumerate(outs):
            for n in range(width // nc):
                c0 = j * width + n * nc
                y = jnp.dot(hn, w_ref[:, c0:c0 + nc], preferred_element_type=F32)
                p_ref[:, n * nc:(n + 1) * nc] = (y * scale if scale != 1.0 else y).astype(p_ref.dtype)


def _mlp_block(x2, pre_g, post_g, w_up, w_dn, layer, *, attn=None, projs=(), tm=512, fc=512, nc=256):
    t, d = x2.shape
    d_ff = w_up.shape[2]
    row_spec = lambda n: pl.BlockSpec((tm, n), lambda i: (i, 0))
    operands, in_specs = [x2], [row_spec(d)]
    est = 2 * 2 * tm * d * 4 + 2 * d * d_ff * 2 + tm * d_ff * 2 + tm * d * 4
    scratch = [pltpu.VMEM((tm, d_ff), BF16), pltpu.VMEM((tm, d), F32)]
    if attn is not None:
        a2, w_o, mix_post_g = attn
        operands += [a2, w_o, mix_post_g]
        in_specs += [row_spec(a2.shape[1]), _const_spec(w_o.shape), _const_spec((1, d))]
        scratch.append(pltpu.VMEM((tm, d), F32))
        est += 2 * tm * a2.shape[1] * 2 + w_o.size * 2 + tm * d * 4
    operands += [pre_g, post_g, w_up, w_dn]
    in_specs += [_const_spec((1, d)), _const_spec((1, d)), _layer_spec(w_up.shape, layer), _layer_spec(w_dn.shape, layer)]
    out_specs, out_shape = [row_spec(d)], [jax.ShapeDtypeStruct((t, d), F32)]
    for g, w, n_out, _ in projs:
        width = w.shape[1] // n_out
        operands += [g, w]
        in_specs += [_const_spec((1, d)), _const_spec(w.shape)]
        out_specs += [row_spec(width)] * n_out
        out_shape += [jax.ShapeDtypeStruct((t, width), BF16)] * n_out
        est += w.size * 2 + n_out * 2 * tm * width * 2
    outs = pl.pallas_call(
        functools.partial(_mlp_kernel, fc=fc, nc=nc, has_attn=attn is not None,
                          proj_outs=tuple(p[2] for p in projs), proj_scales=tuple(p[3] for p in projs)),
        grid=(t // tm,),
        in_specs=in_specs,
        out_specs=out_specs,
        out_shape=out_shape,
        scratch_shapes=scratch,
        compiler_params=pltpu.CompilerParams(
            dimension_semantics=("arbitrary",), vmem_limit_bytes=_vmem_limit(est)),
        name="mlp_block",
    )(*operands)
    rest = list(outs[1:])
    return outs[0], [[rest.pop(0) for _ in range(p[2])] for p in projs]


def _pool_kernel(x_ref, halo_ref, pre_g_ref, post_g_ref, pw_ref, scale_ref, o_ref, hext_ref, lva_ref, lvb_ref,
                 m_ref):
    i = pl.program_id(1)
    tm, d = x_ref.shape[1], x_ref.shape[2]
    gc = d // len(POOL_WINDOWS)
    top = PAD + HALO
    pre_g = pre_g_ref[...]
    h = _rms(x_ref[0], pre_g)
    keep = (i > 0).astype(F32)
    for ref in (hext_ref, lva_ref, lvb_ref):
        ref[0:PAD, :] = jnp.zeros((PAD, ref.shape[1]), F32)
    hext_ref[PAD:top, :] = _rms(halo_ref[0], pre_g) * keep
    hext_ref[top:top + tm, :] = h
    pos = i * tm + lax.broadcasted_iota(jnp.int32, (tm, 1), 0)
    ss = jnp.zeros((tm, 1), F32)
    for g, w in enumerate(POOL_WINDOWS):
        lo, hi = g * gc, (g + 1) * gc
        cur = hext_ref[top:top + tm, lo:hi]
        src, src_cols, dst, span = hext_ref, slice(lo, hi), lva_ref, 1
        while 2 * span < w:
            n = HALO + tm
            dst[PAD:PAD + n, :] = src[PAD:PAD + n, src_cols] + src[PAD - span:PAD - span + n, src_cols]
            src, src_cols, dst = dst, slice(0, gc), (lvb_ref if dst is lva_ref else lva_ref)
            span *= 2
        acc = src[top:top + tm, src_cols] + src[top - span:top - span + tm, src_cols]
        count = jnp.minimum(pos + 1, w).astype(F32)
        y = acc * (1.0 / count) - cur
        m = jnp.dot(y.astype(BF16), pw_ref[g], preferred_element_type=F32) * scale_ref[:, lo:hi]
        m_ref[:, lo:hi] = m
        ss = ss + jnp.sum(m * m, axis=-1, keepdims=True)
    inv = lax.rsqrt(ss * (1.0 / d) + EPS)
    o_ref[0] = x_ref[0] + m_ref[...] * inv * post_g_ref[...]


def _pool_block(x, pre_g, post_g, pw, scale, *, tm=512):
    b, s, d = x.shape
    g, gc = pw.shape[0], pw.shape[1]
    per_tile = tm // HALO
    ext = PAD + HALO + tm
    est = 2 * 2 * tm * d * 4 + 2 * HALO * d * 4 + g * gc * gc * 2 + (ext + tm) * d * 4 + 2 * ext * gc * 4
    return pl.pallas_call(
        _pool_kernel,
        grid=(b, s // tm),
        in_specs=[
            pl.BlockSpec((1, tm, d), lambda bi, i: (bi, i, 0)),
            pl.BlockSpec((1, HALO, d), lambda bi, i: (bi, jnp.maximum(i * per_tile - 1, 0), 0)),
            _const_spec((1, d)),
            _const_spec((1, d)),
            _const_spec((g, gc, gc)),
            _const_spec((1, d)),
        ],
        out_specs=pl.BlockSpec((1, tm, d), lambda bi, i: (bi, i, 0)),
        out_shape=jax.ShapeDtypeStruct((b, s, d), F32),
        scratch_shapes=[pltpu.VMEM((ext, d), F32), pltpu.VMEM((ext, gc), F32), pltpu.VMEM((ext, gc), F32),
                        pltpu.VMEM((tm, d), F32)],
        compiler_params=pltpu.CompilerParams(
            dimension_semantics=("arbitrary", "arbitrary"), vmem_limit_bytes=_vmem_limit(est)),
        name="pool_block",
    )(x, x, pre_g, post_g, pw, scale)


def _softplus2(y):
    return jnp.maximum(y, 0.0) + jnp.log(1.0 + jnp.exp2(-jnp.abs(y))) * LOG2E


def _attn_kernel(q_ref, k_ref, v_ref, cum_ref, o_ref, acc_ref, carry_ref, sp_ref, z_ref, a_ref,
                 *, tq, dh, window, tiles_per_trip):
    s, width = q_ref.shape[1], q_ref.shape[2]
    n_tiles = s // tq
    n_pairs = width // LANES
    row = lax.broadcasted_iota(jnp.int32, (tq, tq), 0)
    col = lax.broadcasted_iota(jnp.int32, (tq, tq), 1)
    causal = col < row
    lane = lax.broadcasted_iota(jnp.int32, (tq, LANES), 1)
    head_lanes = (lane < dh, lane >= dh)

    def old_block(qh, kb, carry):
        z = lax.dot_general(qh, kb, (((1,), (1,)), ((), ())), preferred_element_type=F32)
        sp = _softplus2(z)
        cum = jnp.dot(sp.astype(BF16), cum_ref[0:tq, 0:tq], preferred_element_type=F32)
        return z + cum + carry, carry - jnp.sum(sp, axis=1, keepdims=True)

    def scores(i, n_prev, slot):
        q0 = pl.multiple_of(i * tq, tq)
        k0 = pl.multiple_of((i - n_prev) * tq, tq)
        nb = n_prev + 1
        live = None
        for p in range(n_pairs):
            cols = slice(p * LANES, (p + 1) * LANES)
            qp = q_ref[0, pl.ds(q0, tq), cols]
            kw = k_ref[0, pl.ds(k0, nb * tq), cols]
            for hh in range(2):
                qh = jnp.where(head_lanes[hh], qp, jnp.zeros_like(qp))
                z = lax.dot_general(qh, kw, (((1,), (1,)), ((), ())), preferred_element_type=F32)
                for b in range(nb):
                    r0 = (b * 2 + hh) * tq
                    z_ref[slot, r0:r0 + tq, cols] = z[:, b * tq:(b + 1) * tq]
        for p in range(n_pairs):
            cols = slice(p * LANES, (p + 1) * LANES)
            for hh in range(2):
                carry = None
                for b in range(n_prev, -1, -1):
                    r0 = (b * 2 + hh) * tq
                    zb = z_ref[slot, r0:r0 + tq, cols]
                    if b == n_prev:
                        zb = jnp.where(causal, zb, NEG_BIG)
                    sp = _softplus2(zb)
                    arg = zb if carry is None else zb + carry
                    sp_ref[slot, r0:r0 + tq, cols] = sp.astype(BF16)
                    z_ref[slot, r0:r0 + tq, cols] = arg
                    total = jnp.sum(sp, axis=1, keepdims=True)
                    carry = -total if carry is None else carry - total
                carry_ref[slot, 2 * p + hh] = jnp.broadcast_to(carry, (tq, LANES))
                live = carry if live is None else jnp.maximum(live, carry)
        return jnp.max(live) > DEAD_LOG2

    def weights(n_prev, slot):
        cw = cum_ref.shape[0]
        for b in range(n_prev + 1):
            blk = slice(b * 2 * tq, (b + 1) * 2 * tq)
            for c0 in range(0, width, cw):
                cum = jnp.dot(sp_ref[slot, blk, c0:c0 + cw], cum_ref[...], preferred_element_type=F32)
                a_ref[slot, blk, c0:c0 + cw] = jnp.exp2(z_ref[slot, blk, c0:c0 + cw] + cum).astype(BF16)

    def values(i, n_prev, slot):
        q0 = pl.multiple_of(i * tq, tq)
        k0 = pl.multiple_of((i - n_prev) * tq, tq)
        nb = n_prev + 1
        accs = []
        for p in range(n_pairs):
            cols = slice(p * LANES, (p + 1) * LANES)
            acc = None
            for b in range(nb):
                r0 = b * 2 * tq
                vb = v_ref[0, pl.ds(pl.multiple_of(k0 + b * tq, tq), tq), cols]
                part = jnp.dot(a_ref[slot, r0:r0 + 2 * tq, cols], vb, preferred_element_type=F32)
                acc = part if acc is None else acc + part
            accs.append(acc)
            o_ref[0, pl.ds(q0, tq), cols] = jnp.where(head_lanes[0], acc[:tq], acc[tq:]).astype(o_ref.dtype)
        return accs

    def finish(i, first_old, slot, live, accs):
        def run():
            for p in range(n_pairs):
                for hh in range(2):
                    acc_ref[2 * p + hh] = accs[p][hh * tq:(hh + 1) * tq]

            def cond(st):
                return jnp.logical_and(st[0] >= 0, st[1])

            def body(st):
                j = st[0]
                k0 = pl.multiple_of(j * tq, tq)
                q0 = pl.multiple_of(i * tq, tq)
                mx = None
                for p in range(n_pairs):
                    cols = slice(p * LANES, (p + 1) * LANES)
                    qp = q_ref[0, pl.ds(q0, tq), cols]
                    kb = k_ref[0, pl.ds(k0, tq), cols]
                    vb = v_ref[0, pl.ds(k0, tq), cols]
                    for hh in range(2):
                        h = 2 * p + hh
                        qh = jnp.where(head_lanes[hh], qp, jnp.zeros_like(qp))
                        arg, carry = old_block(qh, kb, carry_ref[slot, h])
                        acc_ref[h] += jnp.dot(jnp.exp2(arg).astype(BF16), vb, preferred_element_type=F32)
                        carry_ref[slot, h] = carry
                        mx = carry if mx is None else jnp.maximum(mx, carry)
                return j - 1, jnp.max(mx) > DEAD_LOG2

            lax.while_loop(cond, body, (first_old, jnp.bool_(True)))
            q0 = pl.multiple_of(i * tq, tq)
            for p in range(n_pairs):
                cols = slice(p * LANES, (p + 1) * LANES)
                o_ref[0, pl.ds(q0, tq), cols] = jnp.where(
                    head_lanes[0], acc_ref[2 * p], acc_ref[2 * p + 1]).astype(o_ref.dtype)

        pl.when(jnp.logical_and(first_old >= 0, live))(run)

    for i in range(window):
        scores(i, i, 0)
        weights(i, 0)
        values(i, i, 0)

    def trip_body(n, c):
        tiles = [window + n * tiles_per_trip + m for m in range(tiles_per_trip)]
        live = [scores(i, window, m) for m, i in enumerate(tiles)]
        for m in range(tiles_per_trip):
            weights(window, m)
        accs = [values(i, window, m) for m, i in enumerate(tiles)]
        for m, i in enumerate(tiles):
            finish(i, i - window - 1, m, live[m], accs[m])
        return c

    lax.fori_loop(0, (n_tiles - window) // tiles_per_trip, trip_body, 0)


def _attention(q, k, v, *, tq=128, window=2, heads_per_step=8, tiles_per_trip=3):
    b, s, hd = q.shape
    dh = hd // N_HEADS
    assert 2 * dh == LANES and tq == LANES
    assert s // tq > window and (s // tq - window) % tiles_per_trip == 0
    width = heads_per_step * dh
    cw = min(width, V7X_MXU_DIM)
    jj = lax.broadcasted_iota(jnp.int32, (cw, cw), 0)
    ss = lax.broadcasted_iota(jnp.int32, (cw, cw), 1)
    cum_mat = jnp.where((jj >= ss) & (jj // tq == ss // tq), -1.0, 0.0).astype(BF16)
    staged_rows = 2 * (window + 1) * tq
    est = (4 * 2 * s * width * 2 + cw * cw * 2 + (1 + tiles_per_trip) * heads_per_step * tq * LANES * 4
           + tiles_per_trip * staged_rows * width * (2 + 4 + 2))
    seq_spec = pl.BlockSpec((1, s, width), lambda bi, g: (bi, 0, g))
    return pl.pallas_call(
        functools.partial(_attn_kernel, tq=tq, dh=dh, window=window, tiles_per_trip=tiles_per_trip),
        grid=(b, hd // width),
        in_specs=[seq_spec, seq_spec, seq_spec, _const_spec((cw, cw))],
        out_specs=seq_spec,
        out_shape=jax.ShapeDtypeStruct((b, s, hd), BF16),
        scratch_shapes=[pltpu.VMEM((heads_per_step, tq, LANES), F32),
                        pltpu.VMEM((tiles_per_trip, heads_per_step, tq, LANES), F32),
                        pltpu.VMEM((tiles_per_trip, staged_rows, width), BF16),
                        pltpu.VMEM((tiles_per_trip, staged_rows, width), F32),
                        pltpu.VMEM((tiles_per_trip, staged_rows, width), BF16)],
        compiler_params=pltpu.CompilerParams(
            dimension_semantics=("arbitrary", "arbitrary"), vmem_limit_bytes=_vmem_limit(est)),
        name="stickbreak_attn",
    )(q, k, v, cum_mat)


def kernel(x, pool_w, pool_scale, w_q, w_kv, kv_norm_g, w_o, w_up, w_down,
           mix_pre_g, mix_post_g, mlp_pre_g, mlp_post_g):
    b, s, d = x.shape
    depth = w_up.shape[0]
    n_a = pool_w.shape[0]
    hd = w_q.shape[2]
    q_scale = float((hd // N_HEADS) ** -0.5) * LOG2E
    assert 1 <= n_a < depth
    row = lambda g: g.reshape(1, d).astype(F32)
    w_up_b, w_down_b = w_up.astype(BF16), w_down.astype(BF16)
    q = k = v = None
    for layer in range(depth):
        attn = None
        if layer < n_a:
            x = _pool_block(x, row(mix_pre_g[layer]), row(mix_post_g[layer]),
                            pool_w[layer].astype(BF16), row(pool_scale[layer]))
        else:
            o = _attention(q.reshape(b, s, hd), k.reshape(b, s, hd), v.reshape(b, s, hd))
            attn = (o.reshape(b * s, hd), w_o[layer - n_a].astype(BF16), row(mix_post_g[layer]))
        projs = []
        if layer == n_a - 1:
            projs.append((row(kv_norm_g), w_kv.astype(BF16), 2, 1.0))
        if n_a <= layer + 1 < depth:
            projs.append((row(mix_pre_g[layer + 1]), w_q[layer + 1 - n_a].astype(BF16), 1, q_scale))
        x2, outs = _mlp_block(x.reshape(b * s, d), row(mlp_pre_g[layer]), row(mlp_post_g[layer]),
                              w_up_b, w_down_b, layer, attn=attn, projs=projs)
        x = x2.reshape(b, s, d)
        if layer == n_a - 1:
            k, v = outs.pop(0)
        if outs:
            (q,) = outs.pop(0)
    return x
```

```python
import functools

import jax
import jax.numpy as jnp
from jax import lax
from jax.experimental import pallas as pl
from jax.experimental.pallas import tpu as pltpu

F32 = jnp.float32
BF16 = jnp.bfloat16

EPS = 1e-6
POOL_WINDOWS = (2, 4, 8, 16)
N_HEADS = 16
HALO = 16
PAD = 8
LANES = 128
V7X_VMEM_BYTES = 64 * 1024 * 1024
V7X_MXU_DIM = 256
NEG_BIG = -1e30
DEAD_LOG2 = -150.0
LOG2E = 1.4426950408889634


def _vmem_limit(est_bytes):
    return int(min(V7X_VMEM_BYTES - 4 * 1024 * 1024, est_bytes * 3 // 2 + 8 * 1024 * 1024))


def _rms(x, g):
    ms = jnp.mean(x * x, axis=-1, keepdims=True)
    return x * lax.rsqrt(ms + EPS) * g


def _const_spec(shape):
    nd = len(shape)
    return pl.BlockSpec(shape, lambda *_: (0,) * nd, pipeline_mode=pl.Buffered(1))


def _layer_spec(stacked_shape, layer):
    nd = len(stacked_shape) - 1
    return pl.BlockSpec((None,) + tuple(stacked_shape[1:]), lambda *_: (layer,) + (0,) * nd,
                        pipeline_mode=pl.Buffered(1))


def _dot_rms_update(lhs_ref, w_ref, g_ref, res_ref, y_ref, dst_ref, nc):
    tm, d = y_ref.shape
    ss = jnp.zeros((tm, 1), F32)
    for n in range(d // nc):
        y = jnp.dot(lhs_ref[...], w_ref[:, n * nc:(n + 1) * nc], preferred_element_type=F32)
        y_ref[:, n * nc:(n + 1) * nc] = y
        ss = ss + jnp.sum(y * y, axis=-1, keepdims=True)
    inv = lax.rsqrt(ss * (1.0 / d) + EPS)
    dst_ref[...] = res_ref[...] + y_ref[...] * inv * g_ref[...]


def _mlp_kernel(*refs, fc, nc, has_attn, proj_outs, proj_scales):
    refs = list(refs)
    x_ref = refs.pop(0)
    if has_attn:
        attn_ref, wo_ref, mix_post_g_ref = refs.pop(0), refs.pop(0), refs.pop(0)
    pre_g_ref, post_g_ref, wup_ref, wdn_ref = (refs.pop(0) for _ in range(4))
    projs = [(refs.pop(0), refs.pop(0)) for _ in proj_outs]
    o_ref = refs.pop(0)
    proj_refs = [[refs.pop(0) for _ in range(n)] for n in proj_outs]
    a_ref, y_ref = refs.pop(0), refs.pop(0)
    d_ff = wup_ref.shape[-1]

    if has_attn:
        x1_ref = refs.pop(0)
        _dot_rms_update(attn_ref, wo_ref, mix_post_g_ref, x_ref, y_ref, x1_ref, nc)
    else:
        x1_ref = x_ref
    h = _rms(x1_ref[...], pre_g_ref[...]).astype(BF16)
    for c in range(d_ff // fc):
        u = jnp.dot(h, wup_ref[:, c * fc:(c + 1) * fc], preferred_element_type=F32)
        r = jnp.maximum(u, 0.0)
        a_ref[:, c * fc:(c + 1) * fc] = (r * r).astype(BF16)
    _dot_rms_update(a_ref, wdn_ref, post_g_ref, x1_ref, y_ref, o_ref, nc)
    for (g_ref, w_ref), outs, scale in zip(projs, proj_refs, proj_scales):
        hn = _rms(o_ref[...], g_ref[...]).astype(BF16)
        width = outs[0].shape[-1]
        for j, p_ref in enumerate(outs):
            for n in range(width // nc):
                c0 = j * width + n * nc
                y = jnp.dot(hn, w_ref[:, c0:c0 + nc], preferred_element_type=F32)
                p_ref[:, n * nc:(n + 1) * nc] = (y * scale if scale != 1.0 else y).astype(p_ref.dtype)


def _mlp_block(x2, pre_g, post_g, w_up, w_dn, layer, *, attn=None, projs=(), tm=512, fc=512, nc=256):
    t, d = x2.shape
    d_ff = w_up.shape[2]
    row_spec = lambda n: pl.BlockSpec((tm, n), lambda i: (i, 0))
    operands, in_specs = [x2], [row_spec(d)]
    est = 2 * 2 * tm * d * 4 + 2 * d * d_ff * 2 + tm * d_ff * 2 + tm * d * 4
    scratch = [pltpu.VMEM((tm, d_ff), BF16), pltpu.VMEM((tm, d), F32)]
    if attn is not None:
        a2, w_o, mix_post_g = attn
        operands += [a2, w_o, mix_post_g]
        in_specs += [row_spec(a2.shape[1]), _const_spec(w_o.shape), _const_spec((1, d))]
        scratch.append(pltpu.VMEM((tm, d), F32))
        est += 2 * tm * a2.shape[1] * 2 + w_o.size * 2 + tm * d * 4
    operands += [pre_g, post_g, w_up, w_dn]
    in_specs += [_const_spec((1, d)), _const_spec((1, d)), _layer_spec(w_up.shape, layer), _layer_spec(w_dn.shape, layer)]
    out_specs, out_shape = [row_spec(d)], [jax.ShapeDtypeStruct((t, d), F32)]
    for g, w, n_out, _ in projs:
        width = w.shape[1] // n_out
        operands += [g, w]
        in_specs += [_const_spec((1, d)), _const_spec(w.shape)]
        out_specs += [row_spec(width)] * n_out
        out_shape += [jax.ShapeDtypeStruct((t, width), BF16)] * n_out
        est += w.size * 2 + n_out * 2 * tm * width * 2
    outs = pl.pallas_call(
        functools.partial(_mlp_kernel, fc=fc, nc=nc, has_attn=attn is not None,
                          proj_outs=tuple(p[2] for p in projs), proj_scales=tuple(p[3] for p in projs)),
        grid=(t // tm,),
        in_specs=in_specs,
        out_specs=out_specs,
        out_shape=out_shape,
        scratch_shapes=scratch,
        compiler_params=pltpu.CompilerParams(
            dimension_semantics=("arbitrary",), vmem_limit_bytes=_vmem_limit(est)),
        name="mlp_block",
    )(*operands)
    rest = list(outs[1:])
    return outs[0], [[rest.pop(0) for _ in range(p[2])] for p in projs]


def _pool_kernel(x_ref, halo_ref, pre_g_ref, post_g_ref, pw_ref, scale_ref, o_ref, hext_ref, lva_ref, lvb_ref,
                 m_ref):
    i = pl.program_id(1)
    tm, d = x_ref.shape[1], x_ref.shape[2]
    gc = d // len(POOL_WINDOWS)
    top = PAD + HALO
    pre_g = pre_g_ref[...]
    h = _rms(x_ref[0], pre_g)
    keep = (i > 0).astype(F32)
    for ref in (hext_ref, lva_ref, lvb_ref):
        ref[0:PAD, :] = jnp.zeros((PAD, ref.shape[1]), F32)
    hext_ref[PAD:top, :] = _rms(halo_ref[0], pre_g) * keep
    hext_ref[top:top + tm, :] = h
    pos = i * tm + lax.broadcasted_iota(jnp.int32, (tm, 1), 0)
    ss = jnp.zeros((tm, 1), F32)
    for g, w in enumerate(POOL_WINDOWS):
        lo, hi = g * gc, (g + 1) * gc
        cur = hext_ref[top:top + tm, lo:hi]
        src, src_cols, dst, span = hext_ref, slice(lo, hi), lva_ref, 1
        while 2 * span < w:
            n = HALO + tm
            dst[PAD:PAD + n, :] = src[PAD:PAD + n, src_cols] + src[PAD - span:PAD - span + n, src_cols]
            src, src_cols, dst = dst, slice(0, gc), (lvb_ref if dst is lva_ref else lva_ref)
            span *= 2
        acc = src[top:top + tm, src_cols] + src[top - span:top - span + tm, src_cols]
        count = jnp.minimum(pos + 1, w).astype(F32)
        y = acc * (1.0 / count) - cur
        m = jnp.dot(y.astype(BF16), pw_ref[g], preferred_element_type=F32) * scale_ref[:, lo:hi]
        m_ref[:, lo:hi] = m
        ss = ss + jnp.sum(m * m, axis=-1, keepdims=True)
    inv = lax.rsqrt(ss * (1.0 / d) + EPS)
    o_ref[0] = x_ref[0] + m_ref[...] * inv * post_g_ref[...]


def _pool_block(x, pre_g, post_g, pw, scale, *, tm=512):
    b, s, d = x.shape
    g, gc = pw.shape[0], pw.shape[1]
    per_tile = tm // HALO
    ext = PAD + HALO + tm
    est = 2 * 2 * tm * d * 4 + 2 * HALO * d * 4 + g * gc * gc * 2 + (ext + tm) * d * 4 + 2 * ext * gc * 4
    return pl.pallas_call(
        _pool_kernel,
        grid=(b, s // tm),
        in_specs=[
            pl.BlockSpec((1, tm, d), lambda bi, i: (bi, i, 0)),
            pl.BlockSpec((1, HALO, d), lambda bi, i: (bi, jnp.maximum(i * per_tile - 1, 0), 0)),
            _const_spec((1, d)),
            _const_spec((1, d)),
            _const_spec((g, gc, gc)),
            _const_spec((1, d)),
        ],
        out_specs=pl.BlockSpec((1, tm, d), lambda bi, i: (bi, i, 0)),
        out_shape=jax.ShapeDtypeStruct((b, s, d), F32),
        scratch_shapes=[pltpu.VMEM((ext, d), F32), pltpu.VMEM((ext, gc), F32), pltpu.VMEM((ext, gc), F32),
                        pltpu.VMEM((tm, d), F32)],
        compiler_params=pltpu.CompilerParams(
            dimension_semantics=("arbitrary", "arbitrary"), vmem_limit_bytes=_vmem_limit(est)),
        name="pool_block",
    )(x, x, pre_g, post_g, pw, scale)


def _softplus2(y):
    return jnp.maximum(y, 0.0) + jnp.log(1.0 + jnp.exp2(-jnp.abs(y))) * LOG2E


def _attn_kernel(q_ref, k_ref, v_ref, cum_ref, o_ref, acc_ref, carry_ref, sp_ref, z_ref, a_ref,
                 *, tq, dh, window, tiles_per_trip):
    s, width = q_ref.shape[1], q_ref.shape[2]
    n_tiles = s // tq
    n_pairs = width // LANES
    row = lax.broadcasted_iota(jnp.int32, (tq, tq), 0)
    col = lax.broadcasted_iota(jnp.int32, (tq, tq), 1)
    causal = col < row
    lane = lax.broadcasted_iota(jnp.int32, (tq, LANES), 1)
    head_lanes = (lane < dh, lane >= dh)

    def old_block(qh, kb, carry):
        z = lax.dot_general(qh, kb, (((1,), (1,)), ((), ())), preferred_element_type=F32)
        sp = _softplus2(z)
        cum = jnp.dot(sp.astype(BF16), cum_ref[0:tq, 0:tq], preferred_element_type=F32)
        return z + cum + carry, carry - jnp.sum(sp, axis=1, keepdims=True)

    def scores(i, n_prev, slot):
        q0 = pl.multiple_of(i * tq, tq)
        k0 = pl.multiple_of((i - n_prev) * tq, tq)
        nb = n_prev + 1
        live = None
        for p in range(n_pairs):
            cols = slice(p * LANES, (p + 1) * LANES)
            qp = q_ref[0, pl.ds(q0, tq), cols]
            kw = k_ref[0, pl.ds(k0, nb * tq), cols]
            for hh in range(2):
                qh = jnp.where(head_lanes[hh], qp, jnp.zeros_like(qp))
                z = lax.dot_general(qh, kw, (((1,), (1,)), ((), ())), preferred_element_type=F32)
                for b in range(nb):
                    r0 = (b * 2 + hh) * tq
                    z_ref[slot, r0:r0 + tq, cols] = z[:, b * tq:(b + 1) * tq]
        for p in range(n_pairs):
            cols = slice(p * LANES, (p + 1) * LANES)
            for hh in range(2):
                carry = None
                for b in range(n_prev, -1, -1):
                    r0 = (b * 2 + hh) * tq
                    zb = z_ref[slot, r0:r0 + tq, cols]
                    if b == n_prev:
                        zb = jnp.where(causal, zb, NEG_BIG)
                    sp = _softplus2(zb)
                    arg = zb if carry is None else zb + carry
                    sp_ref[slot, r0:r0 + tq, cols] = sp.astype(BF16)
                    z_ref[slot, r0:r0 + tq, cols] = arg
                    total = jnp.sum(sp, axis=1, keepdims=True)
                    carry = -total if carry is None else carry - total
                carry_ref[slot, 2 * p + hh] = jnp.broadcast_to(carry, (tq, LANES))
                live = carry if live is None else jnp.maximum(live, carry)
        return jnp.max(live) > DEAD_LOG2

    def weights(n_prev, slot):
        cw = cum_ref.shape[0]
        for b in range(n_prev + 1):
            blk = slice(b * 2 * tq, (b + 1) * 2 * tq)
            for c0 in range(0, width, cw):
                cum = jnp.dot(sp_ref[slot, blk, c0:c0 + cw], cum_ref[...], preferred_element_type=F32)
                a_ref[slot, blk, c0:c0 + cw] = jnp.exp2(z_ref[slot, blk, c0:c0 + cw] + cum).astype(BF16)

    def values(i, n_prev, slot):
        q0 = pl.multiple_of(i * tq, tq)
        k0 = pl.multiple_of((i - n_prev) * tq, tq)
        nb = n_prev + 1
        accs = []
        for p in range(n_pairs):
            cols = slice(p * LANES, (p + 1) * LANES)
            acc = None
            for b in range(nb):
                r0 = b * 2 * tq
                vb = v_ref[0, pl.ds(pl.multiple_of(k0 + b * tq, tq), tq), cols]
                part = jnp.dot(a_ref[slot, r0:r0 + 2 * tq, cols], vb, preferred_element_type=F32)
                acc = part if acc is None else acc + part
            accs.append(acc)
            o_ref[0, pl.ds(q0, tq), cols] = jnp.where(head_lanes[0], acc[:tq], acc[tq:]).astype(o_ref.dtype)
        return accs

    def finish(i, first_old, slot, live, accs):
        def run():
            for p in range(n_pairs):
                for hh in range(2):
                    acc_ref[2 * p + hh] = accs[p][hh * tq:(hh + 1) * tq]

            def cond(st):
                return jnp.logical_and(st[0] >= 0, st[1])

            def body(st):
                j = st[0]
                k0 = pl.multiple_of(j * tq, tq)
                q0 = pl.multiple_of(i * tq, tq)
                mx = None
                for p in range(n_pairs):
                    cols = slice(p * LANES, (p + 1) * LANES)
                    qp = q_ref[0, pl.ds(q0, tq), cols]
                    kb = k_ref[0, pl.ds(k0, tq), cols]
                    vb = v_ref[0, pl.ds(k0, tq), cols]
                    for hh in range(2):
                        h = 2 * p + hh
                        qh = jnp.where(head_lanes[hh], qp, jnp.zeros_like(qp))
                        arg, carry = old_block(qh, kb, carry_ref[slot, h])
                        acc_ref[h] += jnp.dot(jnp.exp2(arg).astype(BF16), vb, preferred_element_type=F32)
                        carry_ref[slot, h] = carry
                        mx = carry if mx is None else jnp.maximum(mx, carry)
                return j - 1, jnp.max(mx) > DEAD_LOG2

            lax.while_loop(cond, body, (first_old, jnp.bool_(True)))
            q0 = pl.multiple_of(i * tq, tq)
            for p in range(n_pairs):
                cols = slice(p * LANES, (p + 1) * LANES)
                o_ref[0, pl.ds(q0, tq), cols] = jnp.where(
                    head_lanes[0], acc_ref[2 * p], acc_ref[2 * p + 1]).astype(o_ref.dtype)

        pl.when(jnp.logical_and(first_old >= 0, live))(run)

    for i in range(window):
        scores(i, i, 0)
        weights(i, 0)
        values(i, i, 0)

    def trip_body(n, c):
        tiles = [window + n * tiles_per_trip + m for m in range(tiles_per_trip)]
        live = [scores(i, window, m) for m, i in enumerate(tiles)]
        for m in range(tiles_per_trip):
            weights(window, m)
        accs = [values(i, window, m) for m, i in enumerate(tiles)]
        for m, i in enumerate(tiles):
            finish(i, i - window - 1, m, live[m], accs[m])
        return c

    lax.fori_loop(0, (n_tiles - window) // tiles_per_trip, trip_body, 0)


def _attention(q, k, v, *, tq=128, window=2, heads_per_step=8, tiles_per_trip=5):
    b, s, hd = q.shape
    dh = hd // N_HEADS
    assert 2 * dh == LANES and tq == LANES
    assert s // tq > window and (s // tq - window) % tiles_per_trip == 0
    width = heads_per_step * dh
    cw = min(width, V7X_MXU_DIM)
    jj = lax.broadcasted_iota(jnp.int32, (cw, cw), 0)
    ss = lax.broadcasted_iota(jnp.int32, (cw, cw), 1)
    cum_mat = jnp.where((jj >= ss) & (jj // tq == ss // tq), -1.0, 0.0).astype(BF16)
    staged_rows = 2 * (window + 1) * tq
    est = (4 * 2 * s * width * 2 + cw * cw * 2 + (1 + tiles_per_trip) * heads_per_step * tq * LANES * 4
           + tiles_per_trip * staged_rows * width * (2 + 4 + 2))
    seq_spec = pl.BlockSpec((1, s, width), lambda bi, g: (bi, 0, g))
    return pl.pallas_call(
        functools.partial(_attn_kernel, tq=tq, dh=dh, window=window, tiles_per_trip=tiles_per_trip),
        grid=(b, hd // width),
        in_specs=[seq_spec, seq_spec, seq_spec, _const_spec((cw, cw))],
        out_specs=seq_spec,
        out_shape=jax.ShapeDtypeStruct((b, s, hd), BF16),
        scratch_shapes=[pltpu.VMEM((heads_per_step, tq, LANES), F32),
                        pltpu.VMEM((tiles_per_trip, heads_per_step, tq, LANES), F32),
                        pltpu.VMEM((tiles_per_trip, staged_rows, width), BF16),
                        pltpu.VMEM((tiles_per_trip, staged_rows, width), F32),
                        pltpu.VMEM((tiles_per_trip, staged_rows, width), BF16)],
        compiler_params=pltpu.CompilerParams(
            dimension_semantics=("arbitrary", "arbitrary"), vmem_limit_bytes=_vmem_limit(est)),
        name="stickbreak_attn",
    )(q, k, v, cum_mat)


def kernel(x, pool_w, pool_scale, w_q, w_kv, kv_norm_g, w_o, w_up, w_down,
           mix_pre_g, mix_post_g, mlp_pre_g, mlp_post_g):
    b, s, d = x.shape
    depth = w_up.shape[0]
    n_a = pool_w.shape[0]
    hd = w_q.shape[2]
    q_scale = float((hd // N_HEADS) ** -0.5) * LOG2E
    assert 1 <= n_a < depth
    row = lambda g: g.reshape(1, d).astype(F32)
    w_up_b, w_down_b = w_up.astype(BF16), w_down.astype(BF16)
    q = k = v = None
    for layer in range(depth):
        attn = None
        if layer < n_a:
            x = _pool_block(x, row(mix_pre_g[layer]), row(mix_post_g[layer]),
                            pool_w[layer].astype(BF16), row(pool_scale[layer]))
        else:
            o = _attention(q.reshape(b, s, hd), k.reshape(b, s, hd), v.reshape(b, s, hd))
            attn = (o.reshape(b * s, hd), w_o[layer - n_a].astype(BF16), row(mix_post_g[layer]))
        projs = []
        if layer == n_a - 1:
            projs.append((row(kv_norm_g), w_kv.astype(BF16), 2, 1.0))
        if n_a <= layer + 1 < depth:
            projs.append((row(mix_pre_g[layer + 1]), w_q[layer + 1 - n_a].astype(BF16), 1, q_scale))
        x2, outs = _mlp_block(x.reshape(b * s, d), row(mlp_pre_g[layer]), row(mlp_post_g[layer]),
                              w_up_b, w_down_b, layer, attn=attn, projs=projs)
        x = x2.reshape(b, s, d)
        if layer == n_a - 1:
            k, v = outs.pop(0)
        if outs:
            (q,) = outs.pop(0)
    return x
```
